```python
import math
import jax, jax.numpy as jnp
from jax import lax
import numpy as np

D_MODEL = 4096
BATCH = 2
SEQ = 4096
DEPTH = 1

CHUNK = 64

SSD_EXPAND = 2
D_INNER = SSD_EXPAND * D_MODEL
SSD_HEAD_DIM = 64
SSD_HEADS = D_INNER // SSD_HEAD_DIM
SSD_STATE = 128
SSD_GROUPS = 8
SSD_HEADS_PER_GROUP = SSD_HEADS // SSD_GROUPS
SSD_CONV = 4
SSD_CONV_DIM = D_INNER + 2 * SSD_GROUPS * SSD_STATE

FOX_HEAD_DIM = 128
FOX_HEADS = D_MODEL // FOX_HEAD_DIM
D_ATT = FOX_HEADS * FOX_HEAD_DIM
Q_BLOCK = 128

D_FF = ((8 * D_MODEL // 3 + 255) // 256) * 256
FFN_CONV = 3

ALPHA = (2.0 * DEPTH) ** 0.25
BETA = (8.0 * DEPTH) ** -0.25
LN_EPS = 1e-5
RMS_EPS = 1e-5

IN_SIZES = (D_INNER, SSD_CONV_DIM, SSD_HEADS, D_ATT, D_ATT, D_ATT, FOX_HEADS, D_MODEL, D_MODEL)
D_IN_PROJ = sum(IN_SIZES)
IN_SPLITS = tuple(int(s) for s in np.cumsum(IN_SIZES)[:-1])

kernel_name = "hybrid_ssd_fox_convffn_deepnorm"


def _layer_norm(x, g, b):
    xf = x.astype(jnp.float32)
    mu = jnp.mean(xf, axis=-1, keepdims=True)
    xc = xf - mu
    var = jnp.mean(xc * xc, axis=-1, keepdims=True)
    out = xc * lax.rsqrt(var + LN_EPS) * g.astype(jnp.float32) + b.astype(jnp.float32)
    return out.astype(x.dtype)


def _causal_dwconv(x, w, b):
    k_width = w.shape[0]
    length = x.shape[1]
    xp = jnp.pad(x, ((0, 0), (k_width - 1, 0), (0, 0)))
    out = b
    for k in range(k_width):
        out = out + w[k] * xp[:, k:k + length]
    return out


def _ssd_branch(z, xbc, dt_raw, conv_w, conv_b, dt_bias, a_log, d_skip, norm_w):
    bsz, length, _ = z.shape
    nc = length // CHUNK
    G, R, P, N = SSD_GROUPS, SSD_HEADS_PER_GROUP, SSD_HEAD_DIM, SSD_STATE
    f32 = jnp.float32
    xbc = jax.nn.silu(_causal_dwconv(xbc, conv_w, conv_b))
    xs, bm, cm = jnp.split(xbc, [D_INNER, D_INNER + G * N], axis=-1)
    xs = xs.astype(f32).reshape(bsz, length, G, R, P)
    bm = bm.astype(f32).reshape(bsz, nc, CHUNK, G, N)
    cm = cm.astype(f32).reshape(bsz, nc, CHUNK, G, N)
    dt = jax.nn.softplus(dt_raw.astype(f32) + dt_bias.astype(f32)).reshape(bsz, length, G, R)
    a = -jnp.exp(a_log.astype(f32)).reshape(G, R)
    da = (dt * a).reshape(bsz, nc, CHUNK, G, R)
    xdt = (xs * dt[..., None]).reshape(bsz, nc, CHUNK, G, R, P)
    acs = jnp.cumsum(da, axis=2)
    seg = acs[:, :, :, None] - acs[:, :, None, :]
    tri = jnp.tril(jnp.ones((CHUNK, CHUNK), dtype=bool))[None, None, :, :, None, None]
    lmat = jnp.exp(jnp.where(tri, seg, -jnp.inf))
    cb = jnp.einsum("bclgn,bcsgn->bclsg", cm, bm)
    y_diag = jnp.einsum("bclsg,bclsgr,bcsgrp->bclgrp", cb, lmat, xdt)
    decay_to_end = jnp.exp(acs[:, :, -1:] - acs)
    states = jnp.einsum("bcsgn,bcsgr,bcsgrp->bcgrpn", bm, decay_to_end, xdt)
    chunk_decay = jnp.exp(acs[:, :, -1])

    def step(h, inp):
        s_c, d_c = inp
        return d_c[..., None, None] * h + s_c, h

    h0 = jnp.zeros((bsz, G, R, P, N), f32)
    _, prev = lax.scan(step, h0, (jnp.moveaxis(states, 1, 0), jnp.moveaxis(chunk_decay, 1, 0)))
    prev = jnp.moveaxis(prev, 0, 1)
    y_off = jnp.einsum("bclgn,bcgrpn,bclgr->bclgrp", cm, prev, jnp.exp(acs))
    y = (y_diag + y_off).reshape(bsz, length, G, R, P) + d_skip.astype(f32).reshape(G, R)[..., None] * xs
    y = y.reshape(bsz, length, D_INNER) * jax.nn.silu(z.astype(f32))
    yg = y.reshape(bsz, length, G, D_INNER // G)
    yg = yg * lax.rsqrt(jnp.mean(yg * yg, axis=-1, keepdims=True) + RMS_EPS)
    return (yg.reshape(bsz, length, D_INNER) * norm_w.astype(f32)).astype(z.dtype)


def _fox_branch(q, k, v, f_logit):
    bsz, length, _ = q.shape
    H, Dh = FOX_HEADS, FOX_HEAD_DIM
    scale = 1.0 / math.sqrt(Dh)
    q = q.reshape(bsz, length, H, Dh).transpose(0, 2, 1, 3)
    k = k.reshape(bsz, length, H, Dh).transpose(0, 2, 1, 3)
    v = v.reshape(bsz, length, H, Dh).transpose(0, 2, 1, 3)
    logf = jax.nn.log_sigmoid(f_logit.astype(jnp.float32))
    fcum = jnp.cumsum(logf, axis=1).transpose(0, 2, 1)
    outs = []
    for i in range(length // Q_BLOCK):
        lo, hi = i * Q_BLOCK, (i + 1) * Q_BLOCK
        s = jnp.einsum("bhqd,bhkd->bhqk", q[:, :, lo:hi], k[:, :, :hi],
                       preferred_element_type=jnp.float32) * scale
        s = s + fcum[:, :, lo:hi, None] - fcum[:, :, None, :hi]
        mask = (lo + jnp.arange(Q_BLOCK))[:, None] >= jnp.arange(hi)[None, :]
        p = jax.nn.softmax(jnp.where(mask, s, -jnp.inf), axis=-1)
        outs.append(jnp.einsum("bhqk,bhkd->bhqd", p.astype(v.dtype), v[:, :, :hi]))
    o = jnp.concatenate(outs, axis=2)
    return o.transpose(0, 2, 1, 3).reshape(bsz, length, D_ATT)


def setup_inputs(seed: int = 0) -> dict:
    key = jax.random.key(seed)
    ks = jax.random.split(key, 24)
    f32 = jnp.float32

    def nrm(k, shape, scale):
        return scale * jax.random.normal(k, shape, f32)

    dt0 = jnp.exp(jax.random.uniform(ks[4], (DEPTH, SSD_HEADS), f32,
                                     minval=math.log(1e-3), maxval=math.log(1e-1)))
    dt_bias = dt0 + jnp.log(-jnp.expm1(-dt0))
    a_log = jnp.log(jax.random.uniform(ks[5], (DEPTH, SSD_HEADS), f32, minval=1.0, maxval=16.0))
    return {
        "x": nrm(ks[0], (BATCH, SEQ, D_MODEL), 1.0),
        "w_in": nrm(ks[1], (DEPTH, D_MODEL, D_IN_PROJ), D_MODEL ** -0.5),
        "ssd_conv_w": nrm(ks[2], (DEPTH, SSD_CONV, SSD_CONV_DIM), SSD_CONV ** -0.5),
        "ssd_conv_b": nrm(ks[3], (DEPTH, SSD_CONV_DIM), 0.01),
        "ssd_dt_bias": dt_bias,
        "ssd_a_log": a_log,
        "ssd_d": 1.0 + nrm(ks[6], (DEPTH, SSD_HEADS), 0.01),
        "ssd_norm_w": 1.0 + nrm(ks[7], (DEPTH, D_INNER), 0.01),
        "fox_f_bias": 3.0 + nrm(ks[8], (DEPTH, FOX_HEADS), 0.5),
        "gate_bias": nrm(ks[9], (DEPTH, 2, D_MODEL), 0.01),
        "w_proj_ssd": nrm(ks[10], (DEPTH, D_INNER, D_MODEL), D_INNER ** -0.5),
        "w_proj_att": nrm(ks[11], (DEPTH, D_ATT, D_MODEL), D_ATT ** -0.5),
        "w_out": nrm(ks[12], (DEPTH, D_MODEL, D_MODEL), BETA * D_MODEL ** -0.5),
        "ln1_g": 1.0 + nrm(ks[13], (DEPTH, D_MODEL), 0.01),
        "ln1_b": nrm(ks[14], (DEPTH, D_MODEL), 0.01),
        "w_up": nrm(ks[15], (DEPTH, D_MODEL, 2 * D_FF), D_MODEL ** -0.5),
        "ffn_conv_w": nrm(ks[16], (DEPTH, FFN_CONV, 2 * D_FF), FFN_CONV ** -0.5),
        "ffn_conv_b": nrm(ks[17], (DEPTH, 2 * D_FF), 0.01),
        "w_down": nrm(ks[18], (DEPTH, D_FF, D_MODEL), BETA * D_FF ** -0.5),
        "ln2_g": 1.0 + nrm(ks[19], (DEPTH, D_MODEL), 0.01),
        "ln2_b": nrm(ks[20], (DEPTH, D_MODEL), 0.01),
    }


def reference(x, w_in, ssd_conv_w, ssd_conv_b, ssd_dt_bias, ssd_a_log, ssd_d, ssd_norm_w,
              fox_f_bias, gate_bias, w_proj_ssd, w_proj_att, w_out, ln1_g, ln1_b,
              w_up, ffn_conv_w, ffn_conv_b, w_down, ln2_g, ln2_b):
    h = x
    for layer in range(DEPTH):
        proj = jnp.einsum("bld,de->ble", h, w_in[layer])
        z, xbc, dt_raw, q, k, v, f_logit, g_ssd, g_att = jnp.split(proj, IN_SPLITS, axis=-1)
        y_ssd = _ssd_branch(z, xbc, dt_raw, ssd_conv_w[layer], ssd_conv_b[layer],
                            ssd_dt_bias[layer], ssd_a_log[layer], ssd_d[layer], ssd_norm_w[layer])
        y_att = _fox_branch(q, k, v, f_logit + fox_f_bias[layer])
        merged = (jax.nn.sigmoid(g_ssd + gate_bias[layer, 0]) * jnp.einsum("ble,ed->bld", y_ssd, w_proj_ssd[layer])
                  + jax.nn.sigmoid(g_att + gate_bias[layer, 1]) * jnp.einsum("ble,ed->bld", y_att, w_proj_att[layer]))
        mix = jnp.einsum("bld,de->ble", merged, w_out[layer])
        h = _layer_norm(ALPHA * h + mix, ln1_g[layer], ln1_b[layer])
        u = _causal_dwconv(jnp.einsum("bld,df->blf", h, w_up[layer]), ffn_conv_w[layer], ffn_conv_b[layer])
        val, gate = jnp.split(u, 2, axis=-1)
        f = jnp.einsum("blf,fd->bld", jax.nn.silu(gate) * val, w_down[layer])
        h = _layer_norm(ALPHA * h + f, ln2_g[layer], ln2_b[layer])
    return h
```

```python
import functools
import math

import jax
import jax.numpy as jnp
from jax import lax
from jax.experimental import pallas as pl
from jax.experimental.pallas import tpu as pltpu

F32 = jnp.float32
BF16 = jnp.bfloat16

SSD_HEAD_DIM = 64
SSD_STATE = 128
FOX_HEAD_DIM = 128
LN_EPS = 1e-5
RMS_EPS = 1e-5

LANES = 128
SUBLANES = 8
BF16_SUBLANES = 16
VMEM_LIMIT_BYTES = 56 * 1024 * 1024

SSD_CHUNK = 128
ATT_BLOCK = 512
CUM_BLOCK = 128


def _cparams(*sem):
    return pltpu.CompilerParams(dimension_semantics=sem, vmem_limit_bytes=VMEM_LIMIT_BYTES)


def _tile(n, pref, quantum=LANES):
    if n <= pref:
        return n
    t = (pref // quantum) * quantum
    while t > quantum and n % t:
        t -= quantum
    assert n % t == 0, (n, pref, quantum)
    return t


def _softplus(x):
    return jnp.maximum(x, 0.0) + jnp.log1p(jnp.exp(-jnp.abs(x)))


def _log_sigmoid(x):
    return jnp.minimum(x, 0.0) - jnp.log1p(jnp.exp(-jnp.abs(x)))


def _silu(x):
    return x * jax.nn.sigmoid(x)


def _mm_body(a_ref, w_ref, o_ref):
    o_ref[...] = jnp.dot(a_ref[...], w_ref[...], preferred_element_type=F32).astype(o_ref.dtype)


def _mm_sigmoid_body(a_ref, w_ref, b_ref, o_ref):
    acc = jnp.dot(a_ref[...], w_ref[...], preferred_element_type=F32)
    o_ref[...] = jax.nn.sigmoid(acc + b_ref[...]).astype(o_ref.dtype)


def _matmul(a, w, out_dtype, *, bias=None, tm=1024, tn=1024, name):
    m, k = a.shape
    n = w.shape[1]
    tm, tn = _tile(m, tm), _tile(n, tn)
    in_specs = [pl.BlockSpec((tm, k), lambda i, j: (i, 0)), pl.BlockSpec((k, tn), lambda i, j: (0, j))]
    args = [a, w]
    body = _mm_body
    if bias is not None:
        in_specs.append(pl.BlockSpec((1, tn), lambda i, j: (0, j)))
        args.append(bias)
        body = _mm_sigmoid_body
    return pl.pallas_call(
        body,
        grid=(m // tm, n // tn),
        in_specs=in_specs,
        out_specs=pl.BlockSpec((tm, tn), lambda i, j: (i, j)),
        out_shape=jax.ShapeDtypeStruct((m, n), out_dtype),
        compiler_params=_cparams("parallel", "arbitrary"),
        name=name,
    )(*args)


def _ssd_body(z_ref, xs_ref, b_ref, c_ref, dt_ref, cwx_ref, cwb_ref, cwc_ref, cbx_ref, cbb_ref, cbc_ref,
              dtb_ref, alog_ref, dcol_ref, nw_ref, y_ref, h_ref, cbuf_ref, ybuf_ref, *, q, r, kw):
    g = pl.program_id(1)
    c = pl.program_id(2)
    gw = r * SSD_HEAD_DIM
    n = SSD_STATE

    @pl.when(c == 0)
    def _():
        h_ref[...] = jnp.zeros_like(h_ref)
        cbuf_ref[0:SUBLANES, :] = jnp.zeros((SUBLANES, cbuf_ref.shape[1]), F32)

    def conv_silu(cur_ref, lo, hi, w_ref, bias_ref):
        cur = cur_ref[...].astype(F32)
        cbuf_ref[SUBLANES:SUBLANES + q, lo:hi] = cur
        w = w_ref[...]
        out = bias_ref[...] + w[kw - 1:kw, :] * cur
        for k in range(kw - 1):
            out = out + w[k:k + 1, :] * cbuf_ref[pl.ds(SUBLANES - (kw - 1 - k), q), lo:hi]
        cbuf_ref[0:SUBLANES, lo:hi] = cur[q - SUBLANES:q, :]
        return _silu(out)

    xs = conv_silu(xs_ref, 0, gw, cwx_ref, cbx_ref)
    bm = conv_silu(b_ref, gw, gw + n, cwb_ref, cbb_ref)
    cm = conv_silu(c_ref, gw + n, gw + 2 * n, cwc_ref, cbc_ref)

    dtv = _softplus(dt_ref[...] + dtb_ref[...])
    da = dtv * (-jnp.exp(alog_ref[...]))
    row = lax.broadcasted_iota(jnp.int32, (q, q), 0)
    col = lax.broadcasted_iota(jnp.int32, (q, q), 1)
    tri = row >= col
    acs = jnp.dot(tri.astype(F32), da, preferred_element_type=F32, precision=lax.Precision.HIGHEST)
    shift = (LANES - g * r) % LANES
    dt_g = pltpu.roll(dtv, shift, axis=1)
    acs_g = pltpu.roll(acs, shift, axis=1)
    dt_t = dt_g.T
    acs_t = acs_g.T

    cm_bf = cm.astype(BF16)
    cb = lax.dot_general(cm_bf, bm.astype(BF16), (((1,), (1,)), ((), ())), preferred_element_type=F32)
    bm_t = bm.T

    lane = lax.broadcasted_iota(jnp.int32, (q, LANES), 1)
    lo_half = lane < SSD_HEAD_DIM
    lo_half_row = lo_half[0:1, :]
    ssq = jnp.zeros((q, 1), F32)
    for j in range(r // 2):
        cols = slice(j * LANES, (j + 1) * LANES)
        xs_p = xs[:, cols]
        rhs = jnp.concatenate([jnp.where(lo_half, xs_p, 0.0).astype(BF16),
                               jnp.where(lo_half, 0.0, xs_p).astype(BF16)], axis=0)
        m_parts, bw_parts, e_cols, cd = [], [], [], []
        for hd in (2 * j, 2 * j + 1):
            a_col = acs_g[:, hd:hd + 1]
            a_row = acs_t[hd:hd + 1, :]
            dt_row = dt_t[hd:hd + 1, :]
            a_last = acs_g[q - 1:q, hd:hd + 1]
            lmat = jnp.exp(jnp.where(tri, a_col - a_row, -jnp.inf))
            m_parts.append(cb * lmat * dt_row)
            bw_parts.append(bm_t * (jnp.exp(a_last - a_row) * dt_row))
            e_cols.append(jnp.exp(a_col))
            cd.append(jnp.exp(a_last))
        y_diag = jnp.dot(jnp.concatenate(m_parts, axis=1).astype(BF16), rhs, preferred_element_type=F32)
        s_new = jnp.dot(jnp.concatenate(bw_parts, axis=1).astype(BF16), rhs, preferred_element_type=F32)
        h_prev = h_ref[:, cols]
        y_off = jnp.dot(cm_bf, h_prev.astype(BF16), preferred_element_type=F32)
        e_pair = jnp.where(lo_half, e_cols[0], e_cols[1])
        cd_pair = jnp.where(lo_half_row, cd[0], cd[1])
        h_ref[:, cols] = cd_pair * h_prev + s_new
        y = y_diag + e_pair * y_off + dcol_ref[:, cols] * xs_p
        y = y * _silu(z_ref[:, cols].astype(F32))
        ybuf_ref[:, cols] = y
        ssq = ssq + jnp.sum(y * y, axis=1, keepdims=True)
    inv = lax.rsqrt(ssq / gw + RMS_EPS)
    y_ref[...] = (ybuf_ref[...] * inv * nw_ref[...]).astype(y_ref.dtype)


def _ssd(zx, small, conv_w, conv_b, dt_bias, a_log, d_cols, norm_w, *, batch, length, d_inner, groups):
    m = zx.shape[0]
    q = SSD_CHUNK
    assert length % q == 0
    nc = length // q
    gw = d_inner // groups
    r = gw // SSD_HEAD_DIM
    assert r % 2 == 0 and r * groups <= LANES
    kw = conv_w.shape[0]
    n = SSD_STATE
    zblk = d_inner // gw
    bblk = 2 * d_inner // n
    cblk_w = d_inner // n
    row = lambda b, g, c: b * nc + c
    in_specs = [
        pl.BlockSpec((q, gw), lambda b, g, c: (row(b, g, c), g)),
        pl.BlockSpec((q, gw), lambda b, g, c: (row(b, g, c), zblk + g)),
        pl.BlockSpec((q, n), lambda b, g, c: (row(b, g, c), bblk + g)),
        pl.BlockSpec((q, n), lambda b, g, c: (row(b, g, c), bblk + groups + g)),
        pl.BlockSpec((q, LANES), lambda b, g, c: (row(b, g, c), 0)),
        pl.BlockSpec((kw, gw), lambda b, g, c: (0, g)),
        pl.BlockSpec((kw, n), lambda b, g, c: (0, cblk_w + g)),
        pl.BlockSpec((kw, n), lambda b, g, c: (0, cblk_w + groups + g)),
        pl.BlockSpec((1, gw), lambda b, g, c: (0, g)),
        pl.BlockSpec((1, n), lambda b, g, c: (0, cblk_w + g)),
        pl.BlockSpec((1, n), lambda b, g, c: (0, cblk_w + groups + g)),
        pl.BlockSpec((1, LANES), lambda b, g, c: (0, 0)),
        pl.BlockSpec((1, LANES), lambda b, g, c: (0, 0)),
        pl.BlockSpec((1, gw), lambda b, g, c: (0, g)),
        pl.BlockSpec((1, gw), lambda b, g, c: (0, g)),
    ]
    return pl.pallas_call(
        functools.partial(_ssd_body, q=q, r=r, kw=kw),
        grid=(batch, groups, nc),
        in_specs=in_specs,
        out_specs=pl.BlockSpec((q, gw), lambda b, g, c: (row(b, g, c), g)),
        out_shape=jax.ShapeDtypeStruct((m, d_inner), BF16),
        scratch_shapes=[
            pltpu.VMEM((n, gw), F32),
            pltpu.VMEM((q + SUBLANES, gw + 2 * n), F32),
            pltpu.VMEM((q, gw), F32),
        ],
        compiler_params=_cparams("parallel", "parallel", "arbitrary"),
        name="ssd_scan",
    )(zx, zx, zx, zx, small, conv_w, conv_w, conv_w, conv_b, conv_b, conv_b, dt_bias, a_log, d_cols, norm_w)


def _fcum_body(f_ref, bias_ref, o_ref, carry_ref, *, heads):
    @pl.when(pl.program_id(1) == 0)
    def _():
        carry_ref[...] = jnp.zeros_like(carry_ref)

    t = f_ref.shape[0]
    logf = _log_sigmoid(f_ref[...] + bias_ref[...])
    row = lax.broadcasted_iota(jnp.int32, (t, t), 0)
    col = lax.broadcasted_iota(jnp.int32, (t, t), 1)
    cs = jnp.dot((row >= col).astype(F32), logf, preferred_element_type=F32,
                 precision=lax.Precision.HIGHEST) + carry_ref[...]
    carry_ref[...] = cs[t - 1:t, :]
    o_ref[...] = cs.T[0:heads, :]


def _fcum(small, f_bias, *, batch, length, heads):
    t = CUM_BLOCK
    nc = length // t
    return pl.pallas_call(
        functools.partial(_fcum_body, heads=heads),
        grid=(batch, nc),
        in_specs=[pl.BlockSpec((t, LANES), lambda b, c: (b * nc + c, 1)),
                  pl.BlockSpec((1, LANES), lambda b, c: (0, 0))],
        out_specs=pl.BlockSpec((None, heads, t), lambda b, c: (b, 0, c)),
        out_shape=jax.ShapeDtypeStruct((batch, heads, length), F32),
        scratch_shapes=[pltpu.VMEM((1, LANES), F32)],
        compiler_params=_cparams("parallel", "arbitrary"),
        name="fox_fcum",
    )(small, f_bias)


def _attn_body(q_ref, k_ref, v_ref, f_ref, o_ref, m_ref, l_ref, acc_ref, *, blk, scale):
    qi = pl.program_id(2)
    m_ref[...] = jnp.full_like(m_ref, -jnp.inf)
    l_ref[...] = jnp.zeros_like(l_ref)
    acc_ref[...] = jnp.zeros_like(acc_ref)
    qv = q_ref[...]

    def step(j, masked):
        start = pl.multiple_of(j * blk, blk)
        kj = k_ref[pl.ds(start, blk), :]
        vj = v_ref[pl.ds(start, blk), :]
        s = lax.dot_general(qv, kj, (((1,), (1,)), ((), ())), preferred_element_type=F32) * scale
        s = s - f_ref[pl.ds(j, 1), :]
        if masked:
            row = lax.broadcasted_iota(jnp.int32, (blk, blk), 0)
            col = lax.broadcasted_iota(jnp.int32, (blk, blk), 1)
            s = jnp.where(row >= col, s, -jnp.inf)
        m_old = m_ref[...]
        m_new = jnp.maximum(m_old, jnp.max(s, axis=1, keepdims=True))
        alpha = jnp.exp(m_old - m_new)
        p = jnp.exp(s - m_new)
        l_ref[...] = alpha * l_ref[...] + jnp.sum(p, axis=1, keepdims=True)
        acc_ref[...] = alpha * acc_ref[...] + jnp.dot(p.astype(BF16), vj, preferred_element_type=F32)
        m_ref[...] = m_new

    def loop_body(j, carry):
        step(j, False)
        return carry

    lax.fori_loop(0, qi, loop_body, 0)
    step(qi, True)
    o_ref[...] = (acc_ref[...] / l_ref[...]).astype(o_ref.dtype)


def _attention(qkv, fcum_t, *, batch, length, heads):
    m = qkv.shape[0]
    dh = FOX_HEAD_DIM
    blk = _tile(length, ATT_BLOCK)
    nq = length // blk
    fcum_t = fcum_t.reshape(batch * heads, nq, blk)
    return pl.pallas_call(
        functools.partial(_attn_body, blk=blk, scale=1.0 / math.sqrt(dh)),
        grid=(batch, heads, nq),
        in_specs=[
            pl.BlockSpec((blk, dh), lambda b, h, i: (b * nq + i, h)),
            pl.BlockSpec((length, dh), lambda b, h, i: (b, heads + h)),
            pl.BlockSpec((length, dh), lambda b, h, i: (b, 2 * heads + h)),
            pl.BlockSpec((None, nq, blk), lambda b, h, i: (b * heads + h, 0, 0)),
        ],
        out_specs=pl.BlockSpec((blk, dh), lambda b, h, i: (b * nq + i, h)),
        out_shape=jax.ShapeDtypeStruct((m, heads * dh), BF16),
        scratch_shapes=[pltpu.VMEM((blk, 1), F32), pltpu.VMEM((blk, 1), F32), pltpu.VMEM((blk, dh), F32)],
        compiler_params=_cparams("parallel", "parallel", "arbitrary"),
        name="fox_attention",
    )(qkv, qkv, qkv, fcum_t)


def _merge_body(ys_ref, ya_ref, ws_ref, wa_ref, gs_ref, ga_ref, o_ref):
    ps = jnp.dot(ys_ref[...], ws_ref[...], preferred_element_type=F32)
    pa = jnp.dot(ya_ref[...], wa_ref[...], preferred_element_type=F32)
    o_ref[...] = (gs_ref[...].astype(F32) * ps + ga_ref[...].astype(F32) * pa).astype(o_ref.dtype)


def _merge(y_ssd, y_att, w_ssd, w_att, gates, *, tm=512, tn=256):
    m, ks = y_ssd.shape
    ka = y_att.shape[1]
    n = w_ssd.shape[1]
    tm, tn = _tile(m, tm), _tile(n, tn)
    nj = n // tn
    return pl.pallas_call(
        _merge_body,
        grid=(m // tm, nj),
        in_specs=[
            pl.BlockSpec((tm, ks), lambda i, j: (i, 0)),
            pl.BlockSpec((tm, ka), lambda i, j: (i, 0)),
            pl.BlockSpec((ks, tn), lambda i, j: (0, j)),
            pl.BlockSpec((ka, tn), lambda i, j: (0, j)),
            pl.BlockSpec((tm, tn), lambda i, j: (i, j)),
            pl.BlockSpec((tm, tn), lambda i, j: (i, nj + j)),
        ],
        out_specs=pl.BlockSpec((tm, tn), lambda i, j: (i, j)),
        out_shape=jax.ShapeDtypeStruct((m, n), BF16),
        compiler_params=_cparams("parallel", "arbitrary"),
        name="merge_proj",
    )(y_ssd, y_att, w_ssd, w_att, gates, gates)


def _mm_residual_body(a_ref, w_ref, r_ref, o_ref, *, alpha):
    acc = jnp.dot(a_ref[...], w_ref[...], preferred_element_type=F32)
    o_ref[...] = alpha * r_ref[...] + acc


def _matmul_residual(a, w, resid, alpha, *, tm, tn, name):
    m, k = a.shape
    n = w.shape[1]
    tm, tn = _tile(m, tm), _tile(n, tn)
    return pl.pallas_call(
        functools.partial(_mm_residual_body, alpha=alpha),
        grid=(m // tm, n // tn),
        in_specs=[
            pl.BlockSpec((tm, k), lambda i, j: (i, 0)),
            pl.BlockSpec((k, tn), lambda i, j: (0, j)),
            pl.BlockSpec((tm, tn), lambda i, j: (i, j)),
        ],
        out_specs=pl.BlockSpec((tm, tn), lambda i, j: (i, j)),
        out_shape=jax.ShapeDtypeStruct((m, n), F32),
        compiler_params=_cparams("parallel", "arbitrary"),
        name=name,
    )(a, w, resid)


def _ln_body(x_ref, g_ref, b_ref, *o_refs):
    x = x_ref[...]
    mu = jnp.mean(x, axis=-1, keepdims=True)
    xc = x - mu
    var = jnp.mean(xc * xc, axis=-1, keepdims=True)
    out = xc * lax.rsqrt(var + LN_EPS) * g_ref[...] + b_ref[...]
    for o_ref in o_refs:
        o_ref[...] = out.astype(o_ref.dtype)


def _layer_norm(x, gain, bias, out_dtypes, *, tm=256, name):
    m, d = x.shape
    tm = _tile(m, tm, SUBLANES)
    outs = pl.pallas_call(
        _ln_body,
        grid=(m // tm,),
        in_specs=[pl.BlockSpec((tm, d), lambda i: (i, 0)),
                  pl.BlockSpec((1, d), lambda i: (0, 0)),
                  pl.BlockSpec((1, d), lambda i: (0, 0))],
        out_specs=[pl.BlockSpec((tm, d), lambda i: (i, 0)) for _ in out_dtypes],
        out_shape=[jax.ShapeDtypeStruct((m, d), dt) for dt in out_dtypes],
        compiler_params=_cparams("parallel"),
        name=name,
    )(x, gain, bias)
    return outs


def _ffn_up_body(a_ref, halo_ref, wv_ref, wg_ref, cwv_ref, cwg_ref, cbv_ref, cbg_ref, o_ref, buf_ref, *,
                 tm, kw, tiles_per_seq):
    i = pl.program_id(0)
    a = a_ref[...]
    halo = halo_ref[...]
    keep = (i % tiles_per_seq != 0).astype(F32)
    pad = BF16_SUBLANES

    def conv(w_ref, cw_ref, cb_ref):
        w = w_ref[...]
        u = jnp.dot(a, w, preferred_element_type=F32)
        uh = jnp.dot(halo, w, preferred_element_type=F32) * keep
        buf_ref[0:pad, :] = uh
        buf_ref[pad:pad + tm, :] = u
        cw = cw_ref[...]
        out = cb_ref[...] + cw[kw - 1:kw, :] * u
        for k in range(kw - 1):
            out = out + cw[k:k + 1, :] * buf_ref[pl.ds(pad - (kw - 1 - k), tm), :]
        return out

    val = conv(wv_ref, cwv_ref, cbv_ref)
    gate = conv(wg_ref, cwg_ref, cbg_ref)
    o_ref[...] = (_silu(gate) * val).astype(o_ref.dtype)


def _ffn_up(h, w_up, conv_w, conv_b, *, length, d_ff, tm=1024, tn=256):
    m, k = h.shape
    tm = _tile(min(m, length), tm, BF16_SUBLANES)
    assert length % tm == 0
    tn = _tile(d_ff, tn)
    nj = d_ff // tn
    kw = conv_w.shape[0]
    hb = tm // BF16_SUBLANES
    return pl.pallas_call(
        functools.partial(_ffn_up_body, tm=tm, kw=kw, tiles_per_seq=length // tm),
        grid=(m // tm, nj),
        in_specs=[
            pl.BlockSpec((tm, k), lambda i, j: (i, 0)),
            pl.BlockSpec((BF16_SUBLANES, k), lambda i, j: (jnp.maximum(i * hb - 1, 0), 0)),
            pl.BlockSpec((k, tn), lambda i, j: (0, j)),
            pl.BlockSpec((k, tn), lambda i, j: (0, nj + j)),
            pl.BlockSpec((kw, tn), lambda i, j: (0, j)),
            pl.BlockSpec((kw, tn), lambda i, j: (0, nj + j)),
            pl.BlockSpec((1, tn), lambda i, j: (0, j)),
            pl.BlockSpec((1, tn), lambda i, j: (0, nj + j)),
        ],
        out_specs=pl.BlockSpec((tm, tn), lambda i, j: (i, j)),
        out_shape=jax.ShapeDtypeStruct((m, d_ff), BF16),
        scratch_shapes=[pltpu.VMEM((tm + BF16_SUBLANES, tn), F32)],
        compiler_params=_cparams("parallel", "arbitrary"),
        name="ffn_up_conv_act",
    )(h, h, w_up, w_up, conv_w, conv_w, conv_b, conv_b)


def _layer(h, p, *, batch, length, alpha):
    m, d = h.shape
    d_inner = p["ssd_norm_w"].shape[-1]
    conv_dim = p["ssd_conv_b"].shape[-1]
    ssd_heads = p["ssd_dt_bias"].shape[-1]
    fox_heads = p["fox_f_bias"].shape[-1]
    d_att = fox_heads * FOX_HEAD_DIM
    d_ff = p["w_down"].shape[0]
    groups = (conv_dim - d_inner) // (2 * SSD_STATE)
    assert ssd_heads <= LANES and fox_heads <= LANES

    o_z, o_xbc = 0, d_inner
    o_dt = o_xbc + conv_dim
    o_q = o_dt + ssd_heads
    o_f = o_q + 3 * d_att
    o_g = o_f + fox_heads
    w_in = p["w_in"]
    w_zx = w_in[:, o_z:o_dt].astype(BF16)
    w_qkv = w_in[:, o_q:o_f].astype(BF16)
    w_gate = w_in[:, o_g:o_g + 2 * d].astype(BF16)
    zeros = lambda n: jnp.zeros((d, n), w_in.dtype)
    w_small = jnp.concatenate([w_in[:, o_dt:o_q], zeros(LANES - ssd_heads),
                               w_in[:, o_f:o_g], zeros(LANES - fox_heads)], axis=1).astype(BF16)

    h_bf = h.astype(BF16)
    zx = _matmul(h_bf, w_zx, F32, name="in_proj_zx")
    qkv = _matmul(h_bf, w_qkv, BF16, name="in_proj_qkv")
    gates = _matmul(h_bf, w_gate, F32, bias=p["gate_bias"].reshape(1, 2 * d), name="in_proj_gates")
    small = _matmul(h_bf, w_small, F32, name="in_proj_small")

    pad_row = lambda v, n: jnp.pad(v.reshape(1, -1).astype(F32), ((0, 0), (0, n - v.shape[-1])))
    y_ssd = _ssd(zx, small, p["ssd_conv_w"], p["ssd_conv_b"].reshape(1, -1),
                 pad_row(p["ssd_dt_bias"], LANES), pad_row(p["ssd_a_log"], LANES),
                 jnp.repeat(p["ssd_d"].astype(F32), SSD_HEAD_DIM).reshape(1, -1),
                 p["ssd_norm_w"].reshape(1, -1),
                 batch=batch, length=length, d_inner=d_inner, groups=groups)

    fcum = _fcum(small, pad_row(p["fox_f_bias"], LANES), batch=batch, length=length, heads=fox_heads)
    y_att = _attention(qkv, fcum, batch=batch, length=length, heads=fox_heads)

    merged = _merge(y_ssd, y_att, p["w_proj_ssd"].astype(BF16), p["w_proj_att"].astype(BF16), gates)
    pre1 = _matmul_residual(merged, p["w_out"].astype(BF16), h, alpha, tm=1024, tn=512, name="out_proj_residual")
    h1, h1_bf = _layer_norm(pre1, p["ln1_g"].reshape(1, -1), p["ln1_b"].reshape(1, -1), (F32, BF16), name="layer_norm_1")

    act = _ffn_up(h1_bf, p["w_up"].astype(BF16), p["ffn_conv_w"], p["ffn_conv_b"].reshape(1, -1),
                  length=length, d_ff=d_ff)
    pre2 = _matmul_residual(act, p["w_down"].astype(BF16), h1, alpha, tm=512, tn=256, name="ffn_down_residual")
    (out,) = _layer_norm(pre2, p["ln2_g"].reshape(1, -1), p["ln2_b"].reshape(1, -1), (F32,), name="layer_norm_2")
    return out


_PARAM_NAMES = ("w_in", "ssd_conv_w", "ssd_conv_b", "ssd_dt_bias", "ssd_a_log", "ssd_d", "ssd_norm_w",
                "fox_f_bias", "gate_bias", "w_proj_ssd", "w_proj_att", "w_out", "ln1_g", "ln1_b",
                "w_up", "ffn_conv_w", "ffn_conv_b", "w_down", "ln2_g", "ln2_b")


def kernel(x, w_in, ssd_conv_w, ssd_conv_b, ssd_dt_bias, ssd_a_log, ssd_d, ssd_norm_w, fox_f_bias, gate_bias,
           w_proj_ssd, w_proj_att, w_out, ln1_g, ln1_b, w_up, ffn_conv_w, ffn_conv_b, w_down, ln2_g, ln2_b):
    params = (w_in, ssd_conv_w, ssd_conv_b, ssd_dt_bias, ssd_a_log, ssd_d, ssd_norm_w, fox_f_bias, gate_bias,
              w_proj_ssd, w_proj_att, w_out, ln1_g, ln1_b, w_up, ffn_conv_w, ffn_conv_b, w_down, ln2_g, ln2_b)
    batch, length, d = x.shape
    depth = w_in.shape[0]
    alpha = (2.0 * depth) ** 0.25
    h = x.reshape(batch * length, d)
    for layer in range(depth):
        p = {name: arr[layer] for name, arr in zip(_PARAM_NAMES, params)}
        h = _layer(h, p, batch=batch, length=length, alpha=alpha)
    return h.reshape(batch, length, d)
```

```python
import functools
import math

import jax
import jax.numpy as jnp
from jax import lax
from jax.experimental import pallas as pl
from jax.experimental.pallas import tpu as pltpu

F32 = jnp.float32
BF16 = jnp.bfloat16

SSD_HEAD_DIM = 64
SSD_STATE = 128
FOX_HEAD_DIM = 128
LN_EPS = 1e-5
RMS_EPS = 1e-5

LANES = 128
SUBLANES = 8
BF16_SUBLANES = 16
VMEM_LIMIT_BYTES = 56 * 1024 * 1024

SSD_CHUNK = 128
ATT_BLOCK = 512
ATT_HEADS_PER_STEP = 4
CUM_BLOCK = 128


def _cparams(*sem):
    return pltpu.CompilerParams(dimension_semantics=sem, vmem_limit_bytes=VMEM_LIMIT_BYTES)


def _tile(n, pref, quantum=LANES):
    if n <= pref:
        return n
    t = (pref // quantum) * quantum
    while t > quantum and n % t:
        t -= quantum
    assert n % t == 0, (n, pref, quantum)
    return t


def _softplus(x):
    return jnp.maximum(x, 0.0) + jnp.log1p(jnp.exp(-jnp.abs(x)))


def _log_sigmoid(x):
    return jnp.minimum(x, 0.0) - jnp.log1p(jnp.exp(-jnp.abs(x)))


def _silu(x):
    return x * jax.nn.sigmoid(x)


def _proj_body(a_ref, w_ref, *rest, scaled_tiles, scale, sigmoid):
    if sigmoid:
        b_ref, o_ref, wbf_ref = rest
    else:
        o_ref, wbf_ref = rest

    @pl.when(pl.program_id(1) == 0)
    def _():
        wbf_ref[...] = w_ref[...].astype(BF16)

    acc = jnp.dot(a_ref[...], wbf_ref[...], preferred_element_type=F32)
    if scaled_tiles:
        acc = acc * jnp.where(pl.program_id(0) < scaled_tiles, scale, 1.0)
    if sigmoid:
        acc = jax.nn.sigmoid(acc + b_ref[...])
    o_ref[...] = acc.astype(o_ref.dtype)


def _proj(a, w, out_dtype, *, n=None, bias=None, scaled_cols=0, scale=1.0, tm=1024, tn=512, name):
    m, k = a.shape
    n = w.shape[1] if n is None else n
    tm, tn = _tile(m, tm), _tile(n, tn)
    assert scaled_cols % tn == 0
    in_specs = [pl.BlockSpec((tm, k), lambda j, i: (i, 0)), pl.BlockSpec((k, tn), lambda j, i: (0, j))]
    args = [a, w]
    if bias is not None:
        in_specs.append(pl.BlockSpec((1, tn), lambda j, i: (0, j)))
        args.append(bias)
    return pl.pallas_call(
        functools.partial(_proj_body, scaled_tiles=scaled_cols // tn, scale=scale, sigmoid=bias is not None),
        grid=(n // tn, m // tm),
        in_specs=in_specs,
        out_specs=pl.BlockSpec((tm, tn), lambda j, i: (i, j)),
        out_shape=jax.ShapeDtypeStruct((m, n), out_dtype),
        scratch_shapes=[pltpu.VMEM((k, tn), BF16)],
        compiler_params=_cparams("parallel", "arbitrary"),
        name=name,
    )(*args)


def _ssd_body(z_ref, xs_ref, b_ref, c_ref, dt_ref, cwx_ref, cwb_ref, cwc_ref, cbx_ref, cbb_ref, cbc_ref,
              dtb_ref, alog_ref, dcol_ref, nw_ref, y_ref, h_ref, cbuf_ref, ybuf_ref, *, q, r, kw):
    g = pl.program_id(1)
    c = pl.program_id(2)
    gw = r * SSD_HEAD_DIM
    n = SSD_STATE

    @pl.when(c == 0)
    def _():
        h_ref[...] = jnp.zeros_like(h_ref)
        cbuf_ref[0:SUBLANES, :] = jnp.zeros((SUBLANES, cbuf_ref.shape[1]), F32)

    def conv_silu(cur_ref, lo, hi, w_ref, bias_ref):
        cur = cur_ref[...].astype(F32)
        cbuf_ref[SUBLANES:SUBLANES + q, lo:hi] = cur
        w = w_ref[...]
        out = bias_ref[...] + w[kw - 1:kw, :] * cur
        for k in range(kw - 1):
            out = out + w[k:k + 1, :] * cbuf_ref[pl.ds(SUBLANES - (kw - 1 - k), q), lo:hi]
        cbuf_ref[0:SUBLANES, lo:hi] = cur[q - SUBLANES:q, :]
        return _silu(out)

    xs = conv_silu(xs_ref, 0, gw, cwx_ref, cbx_ref)
    bm = conv_silu(b_ref, gw, gw + n, cwb_ref, cbb_ref)
    cm = conv_silu(c_ref, gw + n, gw + 2 * n, cwc_ref, cbc_ref)

    dtv = _softplus(dt_ref[...] + dtb_ref[...])
    da = dtv * (-jnp.exp(alog_ref[...]))
    row = lax.broadcasted_iota(jnp.int32, (q, q), 0)
    col = lax.broadcasted_iota(jnp.int32, (q, q), 1)
    tri = row >= col
    acs = jnp.dot(tri.astype(F32), da, preferred_element_type=F32, precision=lax.Precision.HIGHEST)
    shift = (LANES - g * r) % LANES
    dt_g = pltpu.roll(dtv, shift, axis=1)
    acs_g = pltpu.roll(acs, shift, axis=1)
    dt_t = dt_g.T
    acs_t = acs_g.T

    cm_bf = cm.astype(BF16)
    cb = lax.dot_general(cm_bf, bm.astype(BF16), (((1,), (1,)), ((), ())), preferred_element_type=F32)
    bm_t = bm.T

    lane = lax.broadcasted_iota(jnp.int32, (q, LANES), 1)
    lo_half = lane < SSD_HEAD_DIM
    lo_half_row = lo_half[0:1, :]
    ssq = jnp.zeros((q, 1), F32)
    for j in range(r // 2):
        cols = slice(j * LANES, (j + 1) * LANES)
        xs_p = xs[:, cols]
        rhs = jnp.concatenate([jnp.where(lo_half, xs_p, 0.0).astype(BF16),
                               jnp.where(lo_half, 0.0, xs_p).astype(BF16)], axis=0)
        m_parts, bw_parts, e_cols, cd = [], [], [], []
        for hd in (2 * j, 2 * j + 1):
            a_col = acs_g[:, hd:hd + 1]
            a_row = acs_t[hd:hd + 1, :]
            dt_row = dt_t[hd:hd + 1, :]
            a_last = acs_g[q - 1:q, hd:hd + 1]
            lmat = jnp.exp(jnp.where(tri, a_col - a_row, -jnp.inf))
            m_parts.append(cb * lmat * dt_row)
            bw_parts.append(bm_t * (jnp.exp(a_last - a_row) * dt_row))
            e_cols.append(jnp.exp(a_col))
            cd.append(jnp.exp(a_last))
        y_diag = jnp.dot(jnp.concatenate(m_parts, axis=1).astype(BF16), rhs, preferred_element_type=F32)
        s_new = jnp.dot(jnp.concatenate(bw_parts, axis=1).astype(BF16), rhs, preferred_element_type=F32)
        h_prev = h_ref[:, cols]
        y_off = jnp.dot(cm_bf, h_prev.astype(BF16), preferred_element_type=F32)
        e_pair = jnp.where(lo_half, e_cols[0], e_cols[1])
        cd_pair = jnp.where(lo_half_row, cd[0], cd[1])
        h_ref[:, cols] = cd_pair * h_prev + s_new
        y = y_diag + e_pair * y_off + dcol_ref[:, cols] * xs_p
        y = y * _silu(z_ref[:, cols].astype(F32))
        ybuf_ref[:, cols] = y
        ssq = ssq + jnp.sum(y * y, axis=1, keepdims=True)
    inv = lax.rsqrt(ssq / gw + RMS_EPS)
    y_ref[...] = (ybuf_ref[...] * inv * nw_ref[...]).astype(y_ref.dtype)


def _ssd(zx, small, conv_w, conv_b, dt_bias, a_log, d_cols, norm_w, *, batch, length, d_inner, groups):
    m = zx.shape[0]
    q = SSD_CHUNK
    assert length % q == 0
    nc = length // q
    gw = d_inner // groups
    r = gw // SSD_HEAD_DIM
    assert r % 2 == 0 and r * groups <= LANES
    kw = conv_w.shape[0]
    n = SSD_STATE
    zblk = d_inner // gw
    bblk = 2 * d_inner // n
    cblk_w = d_inner // n
    row = lambda b, g, c: b * nc + c
    in_specs = [
        pl.BlockSpec((q, gw), lambda b, g, c: (row(b, g, c), g)),
        pl.BlockSpec((q, gw), lambda b, g, c: (row(b, g, c), zblk + g)),
        pl.BlockSpec((q, n), lambda b, g, c: (row(b, g, c), bblk + g)),
        pl.BlockSpec((q, n), lambda b, g, c: (row(b, g, c), bblk + groups + g)),
        pl.BlockSpec((q, LANES), lambda b, g, c: (row(b, g, c), 0)),
        pl.BlockSpec((kw, gw), lambda b, g, c: (0, g)),
        pl.BlockSpec((kw, n), lambda b, g, c: (0, cblk_w + g)),
        pl.BlockSpec((kw, n), lambda b, g, c: (0, cblk_w + groups + g)),
        pl.BlockSpec((1, gw), lambda b, g, c: (0, g)),
        pl.BlockSpec((1, n), lambda b, g, c: (0, cblk_w + g)),
        pl.BlockSpec((1, n), lambda b, g, c: (0, cblk_w + groups + g)),
        pl.BlockSpec((1, LANES), lambda b, g, c: (0, 0)),
        pl.BlockSpec((1, LANES), lambda b, g, c: (0, 0)),
        pl.BlockSpec((1, gw), lambda b, g, c: (0, g)),
        pl.BlockSpec((1, gw), lambda b, g, c: (0, g)),
    ]
    return pl.pallas_call(
        functools.partial(_ssd_body, q=q, r=r, kw=kw),
        grid=(batch, groups, nc),
        in_specs=in_specs,
        out_specs=pl.BlockSpec((q, gw), lambda b, g, c: (row(b, g, c), g)),
        out_shape=jax.ShapeDtypeStruct((m, d_inner), BF16),
        scratch_shapes=[
            pltpu.VMEM((n, gw), F32),
            pltpu.VMEM((q + SUBLANES, gw + 2 * n), F32),
            pltpu.VMEM((q, gw), F32),
        ],
        compiler_params=_cparams("parallel", "parallel", "arbitrary"),
        name="ssd_scan",
    )(zx, zx, zx, zx, small, conv_w, conv_w, conv_w, conv_b, conv_b, conv_b, dt_bias, a_log, d_cols, norm_w)


def _fcum_body(f_ref, bias_ref, o_ref, carry_ref, *, heads):
    @pl.when(pl.program_id(1) == 0)
    def _():
        carry_ref[...] = jnp.zeros_like(carry_ref)

    t = f_ref.shape[0]
    logf = _log_sigmoid(f_ref[...] + bias_ref[...])
    row = lax.broadcasted_iota(jnp.int32, (t, t), 0)
    col = lax.broadcasted_iota(jnp.int32, (t, t), 1)
    cs = jnp.dot((row >= col).astype(F32), logf, preferred_element_type=F32,
                 precision=lax.Precision.HIGHEST) + carry_ref[...]
    carry_ref[...] = cs[t - 1:t, :]
    o_ref[...] = cs.T[0:heads, :]


def _fcum(small, f_bias, *, batch, length, heads):
    t = CUM_BLOCK
    nc = length // t
    return pl.pallas_call(
        functools.partial(_fcum_body, heads=heads),
        grid=(batch, nc),
        in_specs=[pl.BlockSpec((t, LANES), lambda b, c: (b * nc + c, 1)),
                  pl.BlockSpec((1, LANES), lambda b, c: (0, 0))],
        out_specs=pl.BlockSpec((None, heads, t), lambda b, c: (b, 0, c)),
        out_shape=jax.ShapeDtypeStruct((batch, heads, length), F32),
        scratch_shapes=[pltpu.VMEM((1, LANES), F32)],
        compiler_params=_cparams("parallel", "arbitrary"),
        name="fox_fcum",
    )(small, f_bias)


def _attn_body(q_ref, k_ref, v_ref, f_ref, o_ref, vaug_ref, m_ref, acc_ref, *, blk, hp):
    qi = pl.program_id(2)
    dh = FOX_HEAD_DIM
    length = k_ref.shape[0]

    @pl.when(qi == 0)
    def _():
        for h in range(hp):
            vaug_ref[h, :, 0:dh] = v_ref[:, h * dh:(h + 1) * dh]
            vaug_ref[h, :, dh:2 * dh] = jnp.ones((length, dh), BF16)

    m_ref[...] = jnp.full_like(m_ref, -jnp.inf)
    acc_ref[...] = jnp.zeros_like(acc_ref)

    def step(j, masked):
        start = pl.multiple_of(j * blk, blk)
        for h in range(hp):
            qv = q_ref[:, h * dh:(h + 1) * dh]
            kj = k_ref[pl.ds(start, blk), h * dh:(h + 1) * dh]
            s = lax.dot_general(qv, kj, (((1,), (1,)), ((), ())), preferred_element_type=F32)
            s = s - f_ref[h, pl.ds(j, 1), :]
            if masked:
                row = lax.broadcasted_iota(jnp.int32, (blk, blk), 0)
                col = lax.broadcasted_iota(jnp.int32, (blk, blk), 1)
                s = jnp.where(row >= col, s, -jnp.inf)
            m_old = m_ref[h]
            m_new = jnp.maximum(m_old, jnp.max(s, axis=1, keepdims=True))
            alpha = jnp.exp(m_old - m_new)
            p = jnp.exp(s - jnp.tile(m_new, (1, blk // LANES)))
            pv = jnp.dot(p.astype(BF16), vaug_ref[h, pl.ds(start, blk), :], preferred_element_type=F32)
            acc_ref[h] = jnp.tile(alpha, (1, 2)) * acc_ref[h] + pv
            m_ref[h] = m_new

    def loop_body(j, carry):
        step(j, False)
        return carry

    lax.fori_loop(0, qi, loop_body, 0)
    step(qi, True)
    for h in range(hp):
        acc = acc_ref[h]
        o_ref[:, h * dh:(h + 1) * dh] = (acc[:, 0:dh] / acc[:, dh:2 * dh]).astype(o_ref.dtype)


def _attention(qkv, fcum_t, *, batch, length, heads):
    m = qkv.shape[0]
    dh = FOX_HEAD_DIM
    hp = ATT_HEADS_PER_STEP
    assert dh == LANES and heads % hp == 0
    blk = _tile(length, ATT_BLOCK)
    nq = length // blk
    ng = heads // hp
    fcum_t = fcum_t.reshape(batch * ng, hp, nq, blk)
    return pl.pallas_call(
        functools.partial(_attn_body, blk=blk, hp=hp),
        grid=(batch, ng, nq),
        in_specs=[
            pl.BlockSpec((blk, hp * dh), lambda b, g, i: (b * nq + i, g)),
            pl.BlockSpec((length, hp * dh), lambda b, g, i: (b, ng + g)),
            pl.BlockSpec((length, hp * dh), lambda b, g, i: (b, 2 * ng + g)),
            pl.BlockSpec((None, hp, nq, blk), lambda b, g, i: (b * ng + g, 0, 0, 0)),
        ],
        out_specs=pl.BlockSpec((blk, hp * dh), lambda b, g, i: (b * nq + i, g)),
        out_shape=jax.ShapeDtypeStruct((m, heads * dh), BF16),
        scratch_shapes=[pltpu.VMEM((hp, length, 2 * dh), BF16),
                        pltpu.VMEM((hp, blk, LANES), F32),
                        pltpu.VMEM((hp, blk, 2 * dh), F32)],
        compiler_params=_cparams("parallel", "parallel", "arbitrary"),
        name="fox_attention",
    )(qkv, qkv, qkv, fcum_t)


def _merge_body(ys_ref, ya_ref, ws_ref, wa_ref, gs_ref, ga_ref, o_ref):
    ps = jnp.dot(ys_ref[...], ws_ref[...], preferred_element_type=F32)
    pa = jnp.dot(ya_ref[...], wa_ref[...], preferred_element_type=F32)
    o_ref[...] = (gs_ref[...].astype(F32) * ps + ga_ref[...].astype(F32) * pa).astype(o_ref.dtype)


def _merge(y_ssd, y_att, w_ssd, w_att, gates, *, tm=512, tn=256):
    m, ks = y_ssd.shape
    ka = y_att.shape[1]
    n = w_ssd.shape[1]
    tm, tn = _tile(m, tm), _tile(n, tn)
    nj = n // tn
    return pl.pallas_call(
        _merge_body,
        grid=(m // tm, nj),
        in_specs=[
            pl.BlockSpec((tm, ks), lambda i, j: (i, 0)),
            pl.BlockSpec((tm, ka), lambda i, j: (i, 0)),
            pl.BlockSpec((ks, tn), lambda i, j: (0, j)),
            pl.BlockSpec((ka, tn), lambda i, j: (0, j)),
            pl.BlockSpec((tm, tn), lambda i, j: (i, j)),
            pl.BlockSpec((tm, tn), lambda i, j: (i, nj + j)),
        ],
        out_specs=pl.BlockSpec((tm, tn), lambda i, j: (i, j)),
        out_shape=jax.ShapeDtypeStruct((m, n), BF16),
        compiler_params=_cparams("parallel", "arbitrary"),
        name="merge_proj",
    )(y_ssd, y_att, w_ssd, w_att, gates, gates)


def _mm_residual_body(a_ref, w_ref, r_ref, o_ref, *, alpha):
    acc = jnp.dot(a_ref[...], w_ref[...], preferred_element_type=F32)
    o_ref[...] = alpha * r_ref[...] + acc


def _matmul_residual(a, w, resid, alpha, *, tm, tn, name):
    m, k = a.shape
    n = w.shape[1]
    tm, tn = _tile(m, tm), _tile(n, tn)
    return pl.pallas_call(
        functools.partial(_mm_residual_body, alpha=alpha),
        grid=(m // tm, n // tn),
        in_specs=[
            pl.BlockSpec((tm, k), lambda i, j: (i, 0)),
            pl.BlockSpec((k, tn), lambda i, j: (0, j)),
            pl.BlockSpec((tm, tn), lambda i, j: (i, j)),
        ],
        out_specs=pl.BlockSpec((tm, tn), lambda i, j: (i, j)),
        out_shape=jax.ShapeDtypeStruct((m, n), F32),
        compiler_params=_cparams("parallel", "arbitrary"),
        name=name,
    )(a, w, resid)


def _ln_body(x_ref, g_ref, b_ref, *o_refs):
    x = x_ref[...]
    mu = jnp.mean(x, axis=-1, keepdims=True)
    xc = x - mu
    var = jnp.mean(xc * xc, axis=-1, keepdims=True)
    out = xc * lax.rsqrt(var + LN_EPS) * g_ref[...] + b_ref[...]
    for o_ref in o_refs:
        o_ref[...] = out.astype(o_ref.dtype)


def _layer_norm(x, gain, bias, out_dtypes, *, tm=256, name):
    m, d = x.shape
    tm = _tile(m, tm, SUBLANES)
    outs = pl.pallas_call(
        _ln_body,
        grid=(m // tm,),
        in_specs=[pl.BlockSpec((tm, d), lambda i: (i, 0)),
                  pl.BlockSpec((1, d), lambda i: (0, 0)),
                  pl.BlockSpec((1, d), lambda i: (0, 0))],
        out_specs=[pl.BlockSpec((tm, d), lambda i: (i, 0)) for _ in out_dtypes],
        out_shape=[jax.ShapeDtypeStruct((m, d), dt) for dt in out_dtypes],
        compiler_params=_cparams("parallel"),
        name=name,
    )(x, gain, bias)
    return outs


def _ffn_up_body(a_ref, halo_ref, wv_ref, wg_ref, cwv_ref, cwg_ref, cbv_ref, cbg_ref, o_ref,
                 wbf_ref, buf_ref, *, tm, kw, tiles_per_seq):
    i = pl.program_id(1)
    pad = BF16_SUBLANES

    @pl.when(i == 0)
    def _():
        wbf_ref[0] = wv_ref[...].astype(BF16)
        wbf_ref[1] = wg_ref[...].astype(BF16)

    a = a_ref[...]
    halo = halo_ref[...]
    keep = (i % tiles_per_seq != 0).astype(F32)

    def conv(c, cw_ref, cb_ref):
        w = wbf_ref[c]
        u = jnp.dot(a, w, preferred_element_type=F32)
        buf_ref[c, 0:pad, :] = jnp.dot(halo, w, preferred_element_type=F32) * keep
        buf_ref[c, pad:pad + tm, :] = u
        cw = cw_ref[...]
        out = cb_ref[...] + cw[kw - 1:kw, :] * u
        for k in range(kw - 1):
            out = out + cw[k:k + 1, :] * buf_ref[c, pl.ds(pad - (kw - 1 - k), tm), :]
        return out

    gate = _silu(conv(1, cwg_ref, cbg_ref))
    val = conv(0, cwv_ref, cbv_ref)
    o_ref[...] = (gate * val).astype(o_ref.dtype)


def _ffn_up(h, w_up, conv_w, conv_b, *, length, d_ff, tm=1024, tn=256):
    m, k = h.shape
    tm = _tile(min(m, length), tm, BF16_SUBLANES)
    assert length % tm == 0
    tn = _tile(d_ff, tn)
    nj = d_ff // tn
    kw = conv_w.shape[0]
    hb = tm // BF16_SUBLANES
    return pl.pallas_call(
        functools.partial(_ffn_up_body, tm=tm, kw=kw, tiles_per_seq=length // tm),
        grid=(nj, m // tm),
        in_specs=[
            pl.BlockSpec((tm, k), lambda j, i: (i, 0)),
            pl.BlockSpec((BF16_SUBLANES, k), lambda j, i: (jnp.maximum(i * hb - 1, 0), 0)),
            pl.BlockSpec((k, tn), lambda j, i: (0, j)),
            pl.BlockSpec((k, tn), lambda j, i: (0, nj + j)),
            pl.BlockSpec((kw, tn), lambda j, i: (0, j)),
            pl.BlockSpec((kw, tn), lambda j, i: (0, nj + j)),
            pl.BlockSpec((1, tn), lambda j, i: (0, j)),
            pl.BlockSpec((1, tn), lambda j, i: (0, nj + j)),
        ],
        out_specs=pl.BlockSpec((tm, tn), lambda j, i: (i, j)),
        out_shape=jax.ShapeDtypeStruct((m, d_ff), BF16),
        scratch_shapes=[pltpu.VMEM((2, k, tn), BF16),
                        pltpu.VMEM((2, tm + BF16_SUBLANES, tn), F32)],
        compiler_params=_cparams("parallel", "arbitrary"),
        name="ffn_up_conv_act",
    )(h, h, w_up, w_up, conv_w, conv_w, conv_b, conv_b)


def _layer(h, p, *, batch, length, alpha):
    m, d = h.shape
    d_inner = p["ssd_norm_w"].shape[-1]
    conv_dim = p["ssd_conv_b"].shape[-1]
    ssd_heads = p["ssd_dt_bias"].shape[-1]
    fox_heads = p["fox_f_bias"].shape[-1]
    d_att = fox_heads * FOX_HEAD_DIM
    d_ff = p["w_down"].shape[0]
    groups = (conv_dim - d_inner) // (2 * SSD_STATE)
    assert ssd_heads <= LANES and fox_heads <= LANES

    o_z, o_xbc = 0, d_inner
    o_dt = o_xbc + conv_dim
    o_q = o_dt + ssd_heads
    o_f = o_q + 3 * d_att
    o_g = o_f + fox_heads
    w_in = p["w_in"]
    zeros = lambda n: jnp.zeros((d, n), w_in.dtype)
    w_small = jnp.concatenate([w_in[:, o_dt:o_q], zeros(LANES - ssd_heads),
                               w_in[:, o_f:o_g], zeros(LANES - fox_heads)], axis=1)

    h_bf = h.astype(BF16)
    zx = _proj(h_bf, w_in, F32, n=o_dt, name="in_proj_zx")
    qkv = _proj(h_bf, w_in[:, o_q:o_f], BF16, scaled_cols=d_att, scale=1.0 / math.sqrt(FOX_HEAD_DIM),
                name="in_proj_qkv")
    gates = _proj(h_bf, w_in[:, o_g:o_g + 2 * d], F32, bias=p["gate_bias"].reshape(1, 2 * d), name="in_proj_gates")
    small = _proj(h_bf, w_small, F32, name="in_proj_small")

    pad_row = lambda v, n: jnp.pad(v.reshape(1, -1).astype(F32), ((0, 0), (0, n - v.shape[-1])))
    y_ssd = _ssd(zx, small, p["ssd_conv_w"], p["ssd_conv_b"].reshape(1, -1),
                 pad_row(p["ssd_dt_bias"], LANES), pad_row(p["ssd_a_log"], LANES),
                 jnp.repeat(p["ssd_d"].astype(F32), SSD_HEAD_DIM).reshape(1, -1),
                 p["ssd_norm_w"].reshape(1, -1),
                 batch=batch, length=length, d_inner=d_inner, groups=groups)

    fcum = _fcum(small, pad_row(p["fox_f_bias"], LANES), batch=batch, length=length, heads=fox_heads)
    y_att = _attention(qkv, fcum, batch=batch, length=length, heads=fox_heads)

    merged = _merge(y_ssd, y_att, p["w_proj_ssd"].astype(BF16), p["w_proj_att"].astype(BF16), gates)
    pre1 = _matmul_residual(merged, p["w_out"].astype(BF16), h, alpha, tm=1024, tn=512, name="out_proj_residual")
    h1, h1_bf = _layer_norm(pre1, p["ln1_g"].reshape(1, -1), p["ln1_b"].reshape(1, -1), (F32, BF16), name="layer_norm_1")

    act = _ffn_up(h1_bf, p["w_up"], p["ffn_conv_w"], p["ffn_conv_b"].reshape(1, -1),
                  length=length, d_ff=d_ff)
    pre2 = _matmul_residual(act, p["w_down"].astype(BF16), h1, alpha, tm=512, tn=256, name="ffn_down_residual")
    (out,) = _layer_norm(pre2, p["ln2_g"].reshape(1, -1), p["ln2_b"].reshape(1, -1), (F32,), name="layer_norm_2")
    return out


_PARAM_NAMES = ("w_in", "ssd_conv_w", "ssd_conv_b", "ssd_dt_bias", "ssd_a_log", "ssd_d", "ssd_norm_w",
                "fox_f_bias", "gate_bias", "w_proj_ssd", "w_proj_att", "w_out", "ln1_g", "ln1_b",
                "w_up", "ffn_conv_w", "ffn_conv_b", "w_down", "ln2_g", "ln2_b")


def kernel(x, w_in, ssd_conv_w, ssd_conv_b, ssd_dt_bias, ssd_a_log, ssd_d, ssd_norm_w, fox_f_bias, gate_bias,
           w_proj_ssd, w_proj_att, w_out, ln1_g, ln1_b, w_up, ffn_conv_w, ffn_conv_b, w_down, ln2_g, ln2_b):
    params = (w_in, ssd_conv_w, ssd_conv_b, ssd_dt_bias, ssd_a_log, ssd_d, ssd_norm_w, fox_f_bias, gate_bias,
              w_proj_ssd, w_proj_att, w_out, ln1_g, ln1_b, w_up, ffn_conv_w, ffn_conv_b, w_down, ln2_g, ln2_b)
    batch, length, d = x.shape
    depth = w_in.shape[0]
    alpha = (2.0 * depth) ** 0.25
    h = x.reshape(batch * length, d)
    for layer in range(depth):
        p = {name: arr[layer] for name, arr in zip(_PARAM_NAMES, params)}
        h = _layer(h, p, batch=batch, length=length, alpha=alpha)
    return h.reshape(batch, length, d)
```

```python
import functools
import math

import jax
import jax.numpy as jnp
from jax import lax
from jax.experimental import pallas as pl
from jax.experimental.pallas import tpu as pltpu

F32 = jnp.float32
BF16 = jnp.bfloat16

SSD_HEAD_DIM = 64
SSD_STATE = 128
FOX_HEAD_DIM = 128
LN_EPS = 1e-5
RMS_EPS = 1e-5

LANES = 128
SUBLANES = 8
BF16_SUBLANES = 16
VMEM_LIMIT_BYTES = 56 * 1024 * 1024

SSD_CHUNK = 128
ATT_BLOCK = 512
ATT_HEADS_PER_STEP = 4
CUM_BLOCK = 128


def _cparams(*sem):
    return pltpu.CompilerParams(dimension_semantics=sem, vmem_limit_bytes=VMEM_LIMIT_BYTES)


def _tile(n, pref, quantum=LANES):
    if n <= pref:
        return n
    t = (pref // quantum) * quantum
    while t > quantum and n % t:
        t -= quantum
    assert n % t == 0, (n, pref, quantum)
    return t


def _softplus(x):
    return jnp.maximum(x, 0.0) + jnp.log1p(jnp.exp(-jnp.abs(x)))


def _log_sigmoid(x):
    return jnp.minimum(x, 0.0) - jnp.log1p(jnp.exp(-jnp.abs(x)))


def _silu(x):
    return x * jax.nn.sigmoid(x)


def _proj_body(a_ref, wt_ref, *rest, scaled_tiles, scale, sigmoid):
    if sigmoid:
        b_ref, o_ref = rest
    else:
        (o_ref,) = rest
    acc = lax.dot_general(a_ref[...], wt_ref[...], (((1,), (1,)), ((), ())), preferred_element_type=F32)
    if scaled_tiles:
        acc = acc * jnp.where(pl.program_id(1) < scaled_tiles, scale, 1.0)
    if sigmoid:
        acc = jax.nn.sigmoid(acc + b_ref[...])
    o_ref[...] = acc.astype(o_ref.dtype)


def _proj(a, wt, out_dtype, *, row0=0, n=None, bias=None, scaled_cols=0, scale=1.0, tm=1024, tn=1024, name):
    m, k = a.shape
    n = wt.shape[0] if n is None else n
    tm, tn = _tile(m, tm), _tile(n, tn)
    assert scaled_cols % tn == 0 and row0 % BF16_SUBLANES == 0
    in_specs = [pl.BlockSpec((tm, k), lambda i, j: (i, 0)),
                pl.BlockSpec((pl.Element(tn), pl.Element(k)),
                             lambda i, j: (pl.multiple_of(row0 + j * tn, BF16_SUBLANES), 0))]
    args = [a, wt]
    if bias is not None:
        in_specs.append(pl.BlockSpec((1, tn), lambda i, j: (0, j)))
        args.append(bias)
    return pl.pallas_call(
        functools.partial(_proj_body, scaled_tiles=scaled_cols // tn, scale=scale, sigmoid=bias is not None),
        grid=(m // tm, n // tn),
        in_specs=in_specs,
        out_specs=pl.BlockSpec((tm, tn), lambda i, j: (i, j)),
        out_shape=jax.ShapeDtypeStruct((m, n), out_dtype),
        compiler_params=_cparams("parallel", "arbitrary"),
        name=name,
    )(*args)


def _ssd_body(z_ref, xs_ref, b_ref, c_ref, dt_ref, cwx_ref, cwb_ref, cwc_ref, cbx_ref, cbb_ref, cbc_ref,
              dtb_ref, alog_ref, dcol_ref, nw_ref, y_ref, h_ref, cbuf_ref, ybuf_ref, *, q, r, kw):
    g = pl.program_id(1)
    c = pl.program_id(2)
    gw = r * SSD_HEAD_DIM
    n = SSD_STATE

    @pl.when(c == 0)
    def _():
        h_ref[...] = jnp.zeros_like(h_ref)
        cbuf_ref[0:SUBLANES, :] = jnp.zeros((SUBLANES, cbuf_ref.shape[1]), F32)

    def conv_silu(cur_ref, lo, hi, w_ref, bias_ref):
        cur = cur_ref[...].astype(F32)
        cbuf_ref[SUBLANES:SUBLANES + q, lo:hi] = cur
        w = w_ref[...]
        out = bias_ref[...] + w[kw - 1:kw, :] * cur
        for k in range(kw - 1):
            out = out + w[k:k + 1, :] * cbuf_ref[pl.ds(SUBLANES - (kw - 1 - k), q), lo:hi]
        cbuf_ref[0:SUBLANES, lo:hi] = cur[q - SUBLANES:q, :]
        return _silu(out)

    xs = conv_silu(xs_ref, 0, gw, cwx_ref, cbx_ref)
    bm = conv_silu(b_ref, gw, gw + n, cwb_ref, cbb_ref)
    cm = conv_silu(c_ref, gw + n, gw + 2 * n, cwc_ref, cbc_ref)

    dtv = _softplus(dt_ref[...] + dtb_ref[...])
    da = dtv * (-jnp.exp(alog_ref[...]))
    row = lax.broadcasted_iota(jnp.int32, (q, q), 0)
    col = lax.broadcasted_iota(jnp.int32, (q, q), 1)
    tri = row >= col
    acs = jnp.dot(tri.astype(F32), da, preferred_element_type=F32, precision=lax.Precision.HIGHEST)
    shift = (LANES - g * r) % LANES
    dt_g = pltpu.roll(dtv, shift, axis=1)
    acs_g = pltpu.roll(acs, shift, axis=1)
    dt_t = dt_g.T
    acs_t = acs_g.T

    cm_bf = cm.astype(BF16)
    cb = lax.dot_general(cm_bf, bm.astype(BF16), (((1,), (1,)), ((), ())), preferred_element_type=F32)
    bm_t = bm.T

    lane = lax.broadcasted_iota(jnp.int32, (q, LANES), 1)
    lo_half = lane < SSD_HEAD_DIM
    lo_half_row = lo_half[0:1, :]
    ssq = jnp.zeros((q, 1), F32)
    for j in range(r // 2):
        cols = slice(j * LANES, (j + 1) * LANES)
        xs_p = xs[:, cols]
        rhs = jnp.concatenate([jnp.where(lo_half, xs_p, 0.0).astype(BF16),
                               jnp.where(lo_half, 0.0, xs_p).astype(BF16)], axis=0)
        m_parts, bw_parts, e_cols, cd = [], [], [], []
        for hd in (2 * j, 2 * j + 1):
            a_col = acs_g[:, hd:hd + 1]
            a_row = acs_t[hd:hd + 1, :]
            dt_row = dt_t[hd:hd + 1, :]
            a_last = acs_g[q - 1:q, hd:hd + 1]
            lmat = jnp.exp(jnp.where(tri, a_col - a_row, -jnp.inf))
            m_parts.append(cb * lmat * dt_row)
            bw_parts.append(bm_t * (jnp.exp(a_last - a_row) * dt_row))
            e_cols.append(jnp.exp(a_col))
            cd.append(jnp.exp(a_last))
        y_diag = jnp.dot(jnp.concatenate(m_parts, axis=1).astype(BF16), rhs, preferred_element_type=F32)
        s_new = jnp.dot(jnp.concatenate(bw_parts, axis=1).astype(BF16), rhs, preferred_element_type=F32)
        h_prev = h_ref[:, cols]
        y_off = jnp.dot(cm_bf, h_prev.astype(BF16), preferred_element_type=F32)
        e_pair = jnp.where(lo_half, e_cols[0], e_cols[1])
        cd_pair = jnp.where(lo_half_row, cd[0], cd[1])
        h_ref[:, cols] = cd_pair * h_prev + s_new
        y = y_diag + e_pair * y_off + dcol_ref[:, cols] * xs_p
        y = y * _silu(z_ref[:, cols].astype(F32))
        ybuf_ref[:, cols] = y
        ssq = ssq + jnp.sum(y * y, axis=1, keepdims=True)
    inv = lax.rsqrt(ssq / gw + RMS_EPS)
    y_ref[...] = (ybuf_ref[...] * inv * nw_ref[...]).astype(y_ref.dtype)


def _ssd(zx, small, conv_w, conv_b, dt_bias, a_log, d_cols, norm_w, *, batch, length, d_inner, groups):
    m = zx.shape[0]
    q = SSD_CHUNK
    assert length % q == 0
    nc = length // q
    gw = d_inner // groups
    r = gw // SSD_HEAD_DIM
    assert r % 2 == 0 and r * groups <= LANES
    kw = conv_w.shape[0]
    n = SSD_STATE
    zblk = d_inner // gw
    bblk = 2 * d_inner // n
    cblk_w = d_inner // n
    row = lambda b, g, c: b * nc + c
    in_specs = [
        pl.BlockSpec((q, gw), lambda b, g, c: (row(b, g, c), g)),
        pl.BlockSpec((q, gw), lambda b, g, c: (row(b, g, c), zblk + g)),
        pl.BlockSpec((q, n), lambda b, g, c: (row(b, g, c), bblk + g)),
        pl.BlockSpec((q, n), lambda b, g, c: (row(b, g, c), bblk + groups + g)),
        pl.BlockSpec((q, LANES), lambda b, g, c: (row(b, g, c), 0)),
        pl.BlockSpec((kw, gw), lambda b, g, c: (0, g)),
        pl.BlockSpec((kw, n), lambda b, g, c: (0, cblk_w + g)),
        pl.BlockSpec((kw, n), lambda b, g, c: (0, cblk_w + groups + g)),
        pl.BlockSpec((1, gw), lambda b, g, c: (0, g)),
        pl.BlockSpec((1, n), lambda b, g, c: (0, cblk_w + g)),
        pl.BlockSpec((1, n), lambda b, g, c: (0, cblk_w + groups + g)),
        pl.BlockSpec((1, LANES), lambda b, g, c: (0, 0)),
        pl.BlockSpec((1, LANES), lambda b, g, c: (0, 0)),
        pl.BlockSpec((1, gw), lambda b, g, c: (0, g)),
        pl.BlockSpec((1, gw), lambda b, g, c: (0, g)),
    ]
    return pl.pallas_call(
        functools.partial(_ssd_body, q=q, r=r, kw=kw),
        grid=(batch, groups, nc),
        in_specs=in_specs,
        out_specs=pl.BlockSpec((q, gw), lambda b, g, c: (row(b, g, c), g)),
        out_shape=jax.ShapeDtypeStruct((m, d_inner), BF16),
        scratch_shapes=[
            pltpu.VMEM((n, gw), F32),
            pltpu.VMEM((q + SUBLANES, gw + 2 * n), F32),
            pltpu.VMEM((q, gw), F32),
        ],
        compiler_params=_cparams("parallel", "parallel", "arbitrary"),
        name="ssd_scan",
    )(zx, zx, zx, zx, small, conv_w, conv_w, conv_w, conv_b, conv_b, conv_b, dt_bias, a_log, d_cols, norm_w)


def _fcum_body(f_ref, bias_ref, o_ref, carry_ref, *, heads):
    @pl.when(pl.program_id(1) == 0)
    def _():
        carry_ref[...] = jnp.zeros_like(carry_ref)

    t = f_ref.shape[0]
    logf = _log_sigmoid(f_ref[...] + bias_ref[...])
    row = lax.broadcasted_iota(jnp.int32, (t, t), 0)
    col = lax.broadcasted_iota(jnp.int32, (t, t), 1)
    cs = jnp.dot((row >= col).astype(F32), logf, preferred_element_type=F32,
                 precision=lax.Precision.HIGHEST) + carry_ref[...]
    carry_ref[...] = cs[t - 1:t, :]
    o_ref[...] = cs.T[0:heads, :]


def _fcum(small, f_bias, *, batch, length, heads):
    t = CUM_BLOCK
    nc = length // t
    return pl.pallas_call(
        functools.partial(_fcum_body, heads=heads),
        grid=(batch, nc),
        in_specs=[pl.BlockSpec((t, LANES), lambda b, c: (b * nc + c, 1)),
                  pl.BlockSpec((1, LANES), lambda b, c: (0, 0))],
        out_specs=pl.BlockSpec((None, heads, t), lambda b, c: (b, 0, c)),
        out_shape=jax.ShapeDtypeStruct((batch, heads, length), F32),
        scratch_shapes=[pltpu.VMEM((1, LANES), F32)],
        compiler_params=_cparams("parallel", "arbitrary"),
        name="fox_fcum",
    )(small, f_bias)


def _attn_body(q_ref, k_ref, v_ref, f_ref, o_ref, vaug_ref, m_ref, acc_ref, *, blk, hp):
    qi = pl.program_id(2)
    dh = FOX_HEAD_DIM
    length = k_ref.shape[0]

    @pl.when(qi == 0)
    def _():
        for h in range(hp):
            vaug_ref[h, :, 0:dh] = v_ref[:, h * dh:(h + 1) * dh]
            vaug_ref[h, :, dh:2 * dh] = jnp.ones((length, dh), BF16)

    m_ref[...] = jnp.full_like(m_ref, -jnp.inf)
    acc_ref[...] = jnp.zeros_like(acc_ref)

    def step(j, masked):
        start = pl.multiple_of(j * blk, blk)
        for h in range(hp):
            qv = q_ref[:, h * dh:(h + 1) * dh]
            kj = k_ref[pl.ds(start, blk), h * dh:(h + 1) * dh]
            s = lax.dot_general(qv, kj, (((1,), (1,)), ((), ())), preferred_element_type=F32)
            s = s - f_ref[h, pl.ds(j, 1), :]
            if masked:
                row = lax.broadcasted_iota(jnp.int32, (blk, blk), 0)
                col = lax.broadcasted_iota(jnp.int32, (blk, blk), 1)
                s = jnp.where(row >= col, s, -jnp.inf)
            m_old = m_ref[h]
            m_new = jnp.maximum(m_old, jnp.max(s, axis=1, keepdims=True))
            alpha = jnp.exp(m_old - m_new)
            p = jnp.exp(s - jnp.tile(m_new, (1, blk // LANES)))
            pv = jnp.dot(p.astype(BF16), vaug_ref[h, pl.ds(start, blk), :], preferred_element_type=F32)
            acc_ref[h] = jnp.tile(alpha, (1, 2)) * acc_ref[h] + pv
            m_ref[h] = m_new

    def loop_body(j, carry):
        step(j, False)
        return carry

    lax.fori_loop(0, qi, loop_body, 0)
    step(qi, True)
    for h in range(hp):
        acc = acc_ref[h]
        o_ref[:, h * dh:(h + 1) * dh] = (acc[:, 0:dh] / acc[:, dh:2 * dh]).astype(o_ref.dtype)


def _attention(qkv, fcum_t, *, batch, length, heads):
    m = qkv.shape[0]
    dh = FOX_HEAD_DIM
    hp = ATT_HEADS_PER_STEP
    assert dh == LANES and heads % hp == 0
    blk = _tile(length, ATT_BLOCK)
    nq = length // blk
    ng = heads // hp
    fcum_t = fcum_t.reshape(batch * ng, hp, nq, blk)
    return pl.pallas_call(
        functools.partial(_attn_body, blk=blk, hp=hp),
        grid=(batch, ng, nq),
        in_specs=[
            pl.BlockSpec((blk, hp * dh), lambda b, g, i: (b * nq + i, g)),
            pl.BlockSpec((length, hp * dh), lambda b, g, i: (b, ng + g)),
            pl.BlockSpec((length, hp * dh), lambda b, g, i: (b, 2 * ng + g)),
            pl.BlockSpec((None, hp, nq, blk), lambda b, g, i: (b * ng + g, 0, 0, 0)),
        ],
        out_specs=pl.BlockSpec((blk, hp * dh), lambda b, g, i: (b * nq + i, g)),
        out_shape=jax.ShapeDtypeStruct((m, heads * dh), BF16),
        scratch_shapes=[pltpu.VMEM((hp, length, 2 * dh), BF16),
                        pltpu.VMEM((hp, blk, LANES), F32),
                        pltpu.VMEM((hp, blk, 2 * dh), F32)],
        compiler_params=_cparams("parallel", "parallel", "arbitrary"),
        name="fox_attention",
    )(qkv, qkv, qkv, fcum_t)


def _merge_body(ys_ref, ya_ref, ws_ref, wa_ref, gs_ref, ga_ref, o_ref):
    ps = jnp.dot(ys_ref[...], ws_ref[...], preferred_element_type=F32)
    pa = jnp.dot(ya_ref[...], wa_ref[...], preferred_element_type=F32)
    o_ref[...] = (gs_ref[...].astype(F32) * ps + ga_ref[...].astype(F32) * pa).astype(o_ref.dtype)


def _merge(y_ssd, y_att, w_ssd, w_att, gates, *, tm=512, tn=256):
    m, ks = y_ssd.shape
    ka = y_att.shape[1]
    n = w_ssd.shape[1]
    tm, tn = _tile(m, tm), _tile(n, tn)
    nj = n // tn
    return pl.pallas_call(
        _merge_body,
        grid=(m // tm, nj),
        in_specs=[
            pl.BlockSpec((tm, ks), lambda i, j: (i, 0)),
            pl.BlockSpec((tm, ka), lambda i, j: (i, 0)),
            pl.BlockSpec((ks, tn), lambda i, j: (0, j)),
            pl.BlockSpec((ka, tn), lambda i, j: (0, j)),
            pl.BlockSpec((tm, tn), lambda i, j: (i, j)),
            pl.BlockSpec((tm, tn), lambda i, j: (i, nj + j)),
        ],
        out_specs=pl.BlockSpec((tm, tn), lambda i, j: (i, j)),
        out_shape=jax.ShapeDtypeStruct((m, n), BF16),
        compiler_params=_cparams("parallel", "arbitrary"),
        name="merge_proj",
    )(y_ssd, y_att, w_ssd, w_att, gates, gates)


def _mm_residual_body(a_ref, w_ref, r_ref, o_ref, *, alpha):
    acc = jnp.dot(a_ref[...], w_ref[...], preferred_element_type=F32)
    o_ref[...] = alpha * r_ref[...] + acc


def _matmul_residual(a, w, resid, alpha, *, tm, tn, name):
    m, k = a.shape
    n = w.shape[1]
    tm, tn = _tile(m, tm), _tile(n, tn)
    return pl.pallas_call(
        functools.partial(_mm_residual_body, alpha=alpha),
        grid=(m // tm, n // tn),
        in_specs=[
            pl.BlockSpec((tm, k), lambda i, j: (i, 0)),
            pl.BlockSpec((k, tn), lambda i, j: (0, j)),
            pl.BlockSpec((tm, tn), lambda i, j: (i, j)),
        ],
        out_specs=pl.BlockSpec((tm, tn), lambda i, j: (i, j)),
        out_shape=jax.ShapeDtypeStruct((m, n), F32),
        compiler_params=_cparams("parallel", "arbitrary"),
        name=name,
    )(a, w, resid)


def _ln_body(x_ref, g_ref, b_ref, *o_refs):
    x = x_ref[...]
    mu = jnp.mean(x, axis=-1, keepdims=True)
    xc = x - mu
    var = jnp.mean(xc * xc, axis=-1, keepdims=True)
    out = xc * lax.rsqrt(var + LN_EPS) * g_ref[...] + b_ref[...]
    for o_ref in o_refs:
        o_ref[...] = out.astype(o_ref.dtype)


def _layer_norm(x, gain, bias, out_dtypes, *, tm=256, name):
    m, d = x.shape
    tm = _tile(m, tm, SUBLANES)
    outs = pl.pallas_call(
        _ln_body,
        grid=(m // tm,),
        in_specs=[pl.BlockSpec((tm, d), lambda i: (i, 0)),
                  pl.BlockSpec((1, d), lambda i: (0, 0)),
                  pl.BlockSpec((1, d), lambda i: (0, 0))],
        out_specs=[pl.BlockSpec((tm, d), lambda i: (i, 0)) for _ in out_dtypes],
        out_shape=[jax.ShapeDtypeStruct((m, d), dt) for dt in out_dtypes],
        compiler_params=_cparams("parallel"),
        name=name,
    )(x, gain, bias)
    return outs


def _ffn_up_body(a_ref, halo_ref, wv_ref, wg_ref, cwv_ref, cwg_ref, cbv_ref, cbg_ref, o_ref,
                 wbf_ref, buf_ref, *, tm, kw, tiles_per_seq):
    i = pl.program_id(1)
    pad = BF16_SUBLANES

    @pl.when(i == 0)
    def _():
        wbf_ref[0] = wv_ref[...].astype(BF16)
        wbf_ref[1] = wg_ref[...].astype(BF16)

    a = a_ref[...]
    halo = halo_ref[...]
    keep = (i % tiles_per_seq != 0).astype(F32)

    def conv(c, cw_ref, cb_ref):
        w = wbf_ref[c]
        u = jnp.dot(a, w, preferred_element_type=F32)
        buf_ref[c, 0:pad, :] = jnp.dot(halo, w, preferred_element_type=F32) * keep
        buf_ref[c, pad:pad + tm, :] = u
        cw = cw_ref[...]
        out = cb_ref[...] + cw[kw - 1:kw, :] * u
        for k in range(kw - 1):
            out = out + cw[k:k + 1, :] * buf_ref[c, pl.ds(pad - (kw - 1 - k), tm), :]
        return out

    gate = _silu(conv(1, cwg_ref, cbg_ref))
    val = conv(0, cwv_ref, cbv_ref)
    o_ref[...] = (gate * val).astype(o_ref.dtype)


def _ffn_up(h, w_up, conv_w, conv_b, *, length, d_ff, tm=1024, tn=256):
    m, k = h.shape
    tm = _tile(min(m, length), tm, BF16_SUBLANES)
    assert length % tm == 0
    tn = _tile(d_ff, tn)
    nj = d_ff // tn
    kw = conv_w.shape[0]
    hb = tm // BF16_SUBLANES
    return pl.pallas_call(
        functools.partial(_ffn_up_body, tm=tm, kw=kw, tiles_per_seq=length // tm),
        grid=(nj, m // tm),
        in_specs=[
            pl.BlockSpec((tm, k), lambda j, i: (i, 0)),
            pl.BlockSpec((BF16_SUBLANES, k), lambda j, i: (jnp.maximum(i * hb - 1, 0), 0)),
            pl.BlockSpec((k, tn), lambda j, i: (0, j)),
            pl.BlockSpec((k, tn), lambda j, i: (0, nj + j)),
            pl.BlockSpec((kw, tn), lambda j, i: (0, j)),
            pl.BlockSpec((kw, tn), lambda j, i: (0, nj + j)),
            pl.BlockSpec((1, tn), lambda j, i: (0, j)),
            pl.BlockSpec((1, tn), lambda j, i: (0, nj + j)),
        ],
        out_specs=pl.BlockSpec((tm, tn), lambda j, i: (i, j)),
        out_shape=jax.ShapeDtypeStruct((m, d_ff), BF16),
        scratch_shapes=[pltpu.VMEM((2, k, tn), BF16),
                        pltpu.VMEM((2, tm + BF16_SUBLANES, tn), F32)],
        compiler_params=_cparams("parallel", "arbitrary"),
        name="ffn_up_conv_act",
    )(h, h, w_up, w_up, conv_w, conv_w, conv_b, conv_b)


def _layer(h, p, *, batch, length, alpha):
    m, d = h.shape
    d_inner = p["ssd_norm_w"].shape[-1]
    conv_dim = p["ssd_conv_b"].shape[-1]
    ssd_heads = p["ssd_dt_bias"].shape[-1]
    fox_heads = p["fox_f_bias"].shape[-1]
    d_att = fox_heads * FOX_HEAD_DIM
    d_ff = p["w_down"].shape[0]
    groups = (conv_dim - d_inner) // (2 * SSD_STATE)
    assert ssd_heads <= LANES and fox_heads <= LANES

    o_z, o_xbc = 0, d_inner
    o_dt = o_xbc + conv_dim
    o_q = o_dt + ssd_heads
    o_f = o_q + 3 * d_att
    o_g = o_f + fox_heads
    wt_in = p["w_in"].T.astype(BF16)
    zeros = lambda n: jnp.zeros((n, d), BF16)
    wt_small = jnp.concatenate([wt_in[o_dt:o_q], zeros(LANES - ssd_heads),
                                wt_in[o_f:o_g], zeros(LANES - fox_heads)], axis=0)

    h_bf = h.astype(BF16)
    zx = _proj(h_bf, wt_in, F32, row0=o_z, n=o_dt, name="in_proj_zx")
    qkv = _proj(h_bf, wt_in, BF16, row0=o_q, n=3 * d_att, scaled_cols=d_att, scale=1.0 / math.sqrt(FOX_HEAD_DIM),
                name="in_proj_qkv")
    gates = _proj(h_bf, wt_in, F32, row0=o_g, n=2 * d, bias=p["gate_bias"].reshape(1, 2 * d), name="in_proj_gates")
    small = _proj(h_bf, wt_small, F32, name="in_proj_small")

    pad_row = lambda v, n: jnp.pad(v.reshape(1, -1).astype(F32), ((0, 0), (0, n - v.shape[-1])))
    y_ssd = _ssd(zx, small, p["ssd_conv_w"], p["ssd_conv_b"].reshape(1, -1),
                 pad_row(p["ssd_dt_bias"], LANES), pad_row(p["ssd_a_log"], LANES),
                 jnp.repeat(p["ssd_d"].astype(F32), SSD_HEAD_DIM).reshape(1, -1),
                 p["ssd_norm_w"].reshape(1, -1),
                 batch=batch, length=length, d_inner=d_inner, groups=groups)

    fcum = _fcum(small, pad_row(p["fox_f_bias"], LANES), batch=batch, length=length, heads=fox_heads)
    y_att = _attention(qkv, fcum, batch=batch, length=length, heads=fox_heads)

    merged = _merge(y_ssd, y_att, p["w_proj_ssd"].astype(BF16), p["w_proj_att"].astype(BF16), gates)
    pre1 = _matmul_residual(merged, p["w_out"].astype(BF16), h, alpha, tm=1024, tn=512, name="out_proj_residual")
    h1, h1_bf = _layer_norm(pre1, p["ln1_g"].reshape(1, -1), p["ln1_b"].reshape(1, -1), (F32, BF16), name="layer_norm_1")

    act = _ffn_up(h1_bf, p["w_up"], p["ffn_conv_w"], p["ffn_conv_b"].reshape(1, -1),
                  length=length, d_ff=d_ff)
    pre2 = _matmul_residual(act, p["w_down"].astype(BF16), h1, alpha, tm=512, tn=256, name="ffn_down_residual")
    (out,) = _layer_norm(pre2, p["ln2_g"].reshape(1, -1), p["ln2_b"].reshape(1, -1), (F32,), name="layer_norm_2")
    return out


_PARAM_NAMES = ("w_in", "ssd_conv_w", "ssd_conv_b", "ssd_dt_bias", "ssd_a_log", "ssd_d", "ssd_norm_w",
                "fox_f_bias", "gate_bias", "w_proj_ssd", "w_proj_att", "w_out", "ln1_g", "ln1_b",
                "w_up", "ffn_conv_w", "ffn_conv_b", "w_down", "ln2_g", "ln2_b")


def kernel(x, w_in, ssd_conv_w, ssd_conv_b, ssd_dt_bias, ssd_a_log, ssd_d, ssd_norm_w, fox_f_bias, gate_bias,
           w_proj_ssd, w_proj_att, w_out, ln1_g, ln1_b, w_up, ffn_conv_w, ffn_conv_b, w_down, ln2_g, ln2_b):
    params = (w_in, ssd_conv_w, ssd_conv_b, ssd_dt_bias, ssd_a_log, ssd_d, ssd_norm_w, fox_f_bias, gate_bias,
              w_proj_ssd, w_proj_att, w_out, ln1_g, ln1_b, w_up, ffn_conv_w, ffn_conv_b, w_down, ln2_g, ln2_b)
    batch, length, d = x.shape
    depth = w_in.shape[0]
    alpha = (2.0 * depth) ** 0.25
    h = x.reshape(batch * length, d)
    for layer in range(depth):
        p = {name: arr[layer] for name, arr in zip(_PARAM_NAMES, params)}
        h = _layer(h, p, batch=batch, length=length, alpha=alpha)
    return h.reshape(batch, length, d)
```

```python
import functools
import math

import jax
import jax.numpy as jnp
from jax import lax
from jax.experimental import pallas as pl
from jax.experimental.pallas import tpu as pltpu

F32 = jnp.float32
BF16 = jnp.bfloat16

SSD_HEAD_DIM = 64
SSD_STATE = 128
FOX_HEAD_DIM = 128
LN_EPS = 1e-5
RMS_EPS = 1e-5

LANES = 128
SUBLANES = 8
BF16_SUBLANES = 16
VMEM_LIMIT_BYTES = 56 * 1024 * 1024

SSD_CHUNK = 128
ATT_BLOCK = 512
ATT_HEADS_PER_STEP = 4
CUM_BLOCK = 128


def _cparams(*sem):
    return pltpu.CompilerParams(dimension_semantics=sem, vmem_limit_bytes=VMEM_LIMIT_BYTES)


def _tile(n, pref, quantum=LANES):
    if n <= pref:
        return n
    t = (pref // quantum) * quantum
    while t > quantum and n % t:
        t -= quantum
    assert n % t == 0, (n, pref, quantum)
    return t


def _softplus(x):
    return jnp.maximum(x, 0.0) + jnp.log1p(jnp.exp(-jnp.abs(x)))


def _log_sigmoid(x):
    return jnp.minimum(x, 0.0) - jnp.log1p(jnp.exp(-jnp.abs(x)))


def _silu(x):
    return x * jax.nn.sigmoid(x)


def _proj_body(a_ref, wt_ref, *rest, scaled_tiles, scale, sigmoid):
    if sigmoid:
        b_ref, o_ref = rest
    else:
        (o_ref,) = rest
    acc = lax.dot_general(a_ref[...], wt_ref[...], (((1,), (1,)), ((), ())), preferred_element_type=F32)
    if scaled_tiles:
        acc = acc * jnp.where(pl.program_id(1) < scaled_tiles, scale, 1.0)
    if sigmoid:
        acc = jax.nn.sigmoid(acc + b_ref[...])
    o_ref[...] = acc.astype(o_ref.dtype)


def _proj(a, wt, out_dtype, *, row0=0, n=None, bias=None, scaled_cols=0, scale=1.0, tm=1024, tn=1024, name):
    m, k = a.shape
    n = wt.shape[0] if n is None else n
    tm, tn = _tile(m, tm), _tile(n, tn)
    assert scaled_cols % tn == 0 and row0 % BF16_SUBLANES == 0
    in_specs = [pl.BlockSpec((tm, k), lambda i, j: (i, 0)),
                pl.BlockSpec((pl.Element(tn), pl.Element(k)),
                             lambda i, j: (pl.multiple_of(row0 + j * tn, BF16_SUBLANES), 0))]
    args = [a, wt]
    if bias is not None:
        in_specs.append(pl.BlockSpec((1, tn), lambda i, j: (0, j)))
        args.append(bias)
    return pl.pallas_call(
        functools.partial(_proj_body, scaled_tiles=scaled_cols // tn, scale=scale, sigmoid=bias is not None),
        grid=(m // tm, n // tn),
        in_specs=in_specs,
        out_specs=pl.BlockSpec((tm, tn), lambda i, j: (i, j)),
        out_shape=jax.ShapeDtypeStruct((m, n), out_dtype),
        compiler_params=_cparams("parallel", "arbitrary"),
        name=name,
    )(*args)


def _ssd_prep_body(dt_ref, dtb_ref, alog_ref, acs_ref, dtt_ref, acst_ref):
    q = dt_ref.shape[0]
    dtv = _softplus(dt_ref[...] + dtb_ref[...])
    da = dtv * (-jnp.exp(alog_ref[...]))
    row = lax.broadcasted_iota(jnp.int32, (q, q), 0)
    col = lax.broadcasted_iota(jnp.int32, (q, q), 1)
    acs = jnp.dot((row >= col).astype(F32), da, preferred_element_type=F32, precision=lax.Precision.HIGHEST)
    acs_ref[...] = acs
    dtt_ref[...] = dtv.T
    acst_ref[...] = acs.T


def _ssd_prep(small, dt_bias, a_log):
    m = small.shape[0]
    q = SSD_CHUNK
    nat = pl.BlockSpec((q, LANES), lambda i: (i, 0))
    tr = pl.BlockSpec((LANES, q), lambda i: (0, i))
    vec = pl.BlockSpec((1, LANES), lambda i: (0, 0))
    return pl.pallas_call(
        _ssd_prep_body,
        grid=(m // q,),
        in_specs=[nat, vec, vec],
        out_specs=[nat, tr, tr],
        out_shape=[jax.ShapeDtypeStruct((m, LANES), F32), jax.ShapeDtypeStruct((LANES, m), F32),
                   jax.ShapeDtypeStruct((LANES, m), F32)],
        compiler_params=_cparams("parallel"),
        name="ssd_dt_prep",
    )(small, dt_bias, a_log)


def _ssd_body(z_ref, xs_ref, b_ref, c_ref, acs_ref, dtt_ref, acst_ref, px_ref, pb_ref, pc_ref, y_ref,
              h_ref, tail_ref, ybuf_ref, *, q, r, kw):
    g = pl.program_id(1)
    c = pl.program_id(2)
    gw = r * SSD_HEAD_DIM
    n = SSD_STATE

    @pl.when(c == 0)
    def _():
        h_ref[...] = jnp.zeros_like(h_ref)
        tail_ref[...] = jnp.zeros_like(tail_ref)

    def conv_silu(cur_ref, lo, hi, p_ref):
        cur = cur_ref[...].astype(F32)
        p = p_ref[...]
        bias = p[kw:kw + 1, :]
        out = bias + p[kw - 1:kw, :] * cur
        head_src = jnp.concatenate([tail_ref[:, lo:hi], cur[0:SUBLANES, :]], axis=0)
        head = bias + p[kw - 1:kw, :] * head_src[SUBLANES:2 * SUBLANES, :]
        for k in range(kw - 1):
            shift = kw - 1 - k
            out = out + p[k:k + 1, :] * pltpu.roll(cur, shift, axis=0)
            head = head + p[k:k + 1, :] * head_src[SUBLANES - shift:2 * SUBLANES - shift, :]
        tail_ref[:, lo:hi] = cur[q - SUBLANES:q, :]
        return _silu(jnp.concatenate([head, out[SUBLANES:, :]], axis=0))

    xs = conv_silu(xs_ref, 0, gw, px_ref)
    bm = conv_silu(b_ref, gw, gw + n, pb_ref)
    cm = conv_silu(c_ref, gw + n, gw + 2 * n, pc_ref)

    row = lax.broadcasted_iota(jnp.int32, (q, q), 0)
    col = lax.broadcasted_iota(jnp.int32, (q, q), 1)
    tri = row >= col
    acs_g = pltpu.roll(acs_ref[...], (LANES - g * r) % LANES, axis=1)
    dt_t = dtt_ref[...]
    acs_t = acst_ref[...]

    cm_bf = cm.astype(BF16)
    cb = lax.dot_general(cm_bf, bm.astype(BF16), (((1,), (1,)), ((), ())), preferred_element_type=F32)
    bm_t = bm.T

    lane = lax.broadcasted_iota(jnp.int32, (q, LANES), 1)
    lo_half = lane < SSD_HEAD_DIM
    lo_half_row = lo_half[0:1, :]
    pairs = range(r // 2)
    pair_cols = [slice(j * LANES, (j + 1) * LANES) for j in pairs]
    y_off = jnp.dot(cm_bf, h_ref[...].astype(BF16), preferred_element_type=F32)

    lhs_y, lhs_s, rhs, e_pair, cd_pair = [], [], [], [], []
    for j in pairs:
        xs_p = xs[:, pair_cols[j]]
        rhs.append(jnp.concatenate([jnp.where(lo_half, xs_p, 0.0).astype(BF16),
                                    jnp.where(lo_half, 0.0, xs_p).astype(BF16)], axis=0))
        m_parts, bw_parts, e_cols, cd = [], [], [], []
        for hd in (2 * j, 2 * j + 1):
            a_col = acs_g[:, hd:hd + 1]
            a_row = acs_t[hd:hd + 1, :]
            dt_row = dt_t[hd:hd + 1, :]
            a_last = acs_g[q - 1:q, hd:hd + 1]
            lmat = jnp.exp(jnp.where(tri, a_col - a_row, -jnp.inf))
            m_parts.append(cb * lmat * dt_row)
            bw_parts.append(bm_t * (jnp.exp(a_last - a_row) * dt_row))
            e_cols.append(jnp.exp(a_col))
            cd.append(jnp.exp(a_last))
        lhs_y.append(jnp.concatenate(m_parts, axis=1).astype(BF16))
        lhs_s.append(jnp.concatenate(bw_parts, axis=1).astype(BF16))
        e_pair.append(jnp.where(lo_half, e_cols[0], e_cols[1]))
        cd_pair.append(jnp.where(lo_half_row, cd[0], cd[1]))
    y_diag = [jnp.dot(lhs_y[j], rhs[j], preferred_element_type=F32) for j in pairs]
    s_new = [jnp.dot(lhs_s[j], rhs[j], preferred_element_type=F32) for j in pairs]
    ssq = jnp.zeros((q, 1), F32)
    for j in pairs:
        cols = pair_cols[j]
        h_ref[:, cols] = cd_pair[j] * h_ref[:, cols] + s_new[j]
        y = y_diag[j] + e_pair[j] * y_off[:, cols] + px_ref[kw + 1:kw + 2, cols] * xs[:, cols]
        y = y * _silu(z_ref[:, cols].astype(F32))
        ybuf_ref[:, cols] = y
        ssq = ssq + jnp.sum(y * y, axis=1, keepdims=True)
    inv = lax.rsqrt(ssq / gw + RMS_EPS)
    y_ref[...] = (ybuf_ref[...] * inv * px_ref[kw + 2:kw + 3, :]).astype(y_ref.dtype)


def _ssd(zx, small, conv_w, conv_b, dt_bias, a_log, d_cols, norm_w, *, batch, length, d_inner, groups):
    m = zx.shape[0]
    q = SSD_CHUNK
    assert length % q == 0
    nc = length // q
    gw = d_inner // groups
    r = gw // SSD_HEAD_DIM
    assert r % 2 == 0 and r % SUBLANES == 0 and r * groups <= LANES
    kw = conv_w.shape[0]
    n = SSD_STATE
    acs, dt_t, acs_t = _ssd_prep(small, dt_bias, a_log)
    conv_dim = conv_w.shape[1]
    rows = jnp.concatenate([conv_w, conv_b], axis=0)
    extra = jnp.zeros((2, conv_dim), F32).at[0, :d_inner].set(d_cols[0]).at[1, :d_inner].set(norm_w[0])
    params = jnp.concatenate([rows, extra], axis=0)
    np_rows = params.shape[0]
    zblk = d_inner // gw
    bblk = 2 * d_inner // n
    cblk_w = d_inner // n
    row = lambda b, g, c: b * nc + c
    in_specs = [
        pl.BlockSpec((q, gw), lambda b, g, c: (row(b, g, c), g)),
        pl.BlockSpec((q, gw), lambda b, g, c: (row(b, g, c), zblk + g)),
        pl.BlockSpec((q, n), lambda b, g, c: (row(b, g, c), bblk + g)),
        pl.BlockSpec((q, n), lambda b, g, c: (row(b, g, c), bblk + groups + g)),
        pl.BlockSpec((q, LANES), lambda b, g, c: (row(b, g, c), 0)),
        pl.BlockSpec((r, q), lambda b, g, c: (g, row(b, g, c))),
        pl.BlockSpec((r, q), lambda b, g, c: (g, row(b, g, c))),
        pl.BlockSpec((np_rows, gw), lambda b, g, c: (0, g)),
        pl.BlockSpec((np_rows, n), lambda b, g, c: (0, cblk_w + g)),
        pl.BlockSpec((np_rows, n), lambda b, g, c: (0, cblk_w + groups + g)),
    ]
    return pl.pallas_call(
        functools.partial(_ssd_body, q=q, r=r, kw=kw),
        grid=(batch, groups, nc),
        in_specs=in_specs,
        out_specs=pl.BlockSpec((q, gw), lambda b, g, c: (row(b, g, c), g)),
        out_shape=jax.ShapeDtypeStruct((m, d_inner), BF16),
        scratch_shapes=[
            pltpu.VMEM((n, gw), F32),
            pltpu.VMEM((SUBLANES, gw + 2 * n), F32),
            pltpu.VMEM((q, gw), F32),
        ],
        compiler_params=_cparams("parallel", "parallel", "arbitrary"),
        name="ssd_scan",
    )(zx, zx, zx, zx, acs, dt_t, acs_t, params, params, params)


def _fcum_body(f_ref, bias_ref, o_ref, carry_ref, *, heads):
    @pl.when(pl.program_id(1) == 0)
    def _():
        carry_ref[...] = jnp.zeros_like(carry_ref)

    t = f_ref.shape[0]
    logf = _log_sigmoid(f_ref[...] + bias_ref[...])
    row = lax.broadcasted_iota(jnp.int32, (t, t), 0)
    col = lax.broadcasted_iota(jnp.int32, (t, t), 1)
    cs = jnp.dot((row >= col).astype(F32), logf, preferred_element_type=F32,
                 precision=lax.Precision.HIGHEST) + carry_ref[...]
    carry_ref[...] = cs[t - 1:t, :]
    o_ref[...] = cs.T[0:heads, :]


def _fcum(small, f_bias, *, batch, length, heads):
    t = CUM_BLOCK
    nc = length // t
    return pl.pallas_call(
        functools.partial(_fcum_body, heads=heads),
        grid=(batch, nc),
        in_specs=[pl.BlockSpec((t, LANES), lambda b, c: (b * nc + c, 1)),
                  pl.BlockSpec((1, LANES), lambda b, c: (0, 0))],
        out_specs=pl.BlockSpec((None, heads, t), lambda b, c: (b, 0, c)),
        out_shape=jax.ShapeDtypeStruct((batch, heads, length), F32),
        scratch_shapes=[pltpu.VMEM((1, LANES), F32)],
        compiler_params=_cparams("parallel", "arbitrary"),
        name="fox_fcum",
    )(small, f_bias)


def _attn_body(q_ref, k_ref, v_ref, f_ref, o_ref, vaug_ref, m_ref, acc_ref, *, blk, hp):
    qi = pl.program_id(2)
    dh = FOX_HEAD_DIM
    length = k_ref.shape[0]

    @pl.when(qi == 0)
    def _():
        for h in range(hp):
            vaug_ref[h, :, 0:dh] = v_ref[:, h * dh:(h + 1) * dh]
            vaug_ref[h, :, dh:2 * dh] = jnp.ones((length, dh), BF16)

    m_ref[...] = jnp.full_like(m_ref, -jnp.inf)
    acc_ref[...] = jnp.zeros_like(acc_ref)

    def step(j, masked):
        start = pl.multiple_of(j * blk, blk)

        def scores(h):
            qv = q_ref[:, h * dh:(h + 1) * dh]
            kj = k_ref[pl.ds(start, blk), h * dh:(h + 1) * dh]
            return lax.dot_general(qv, kj, (((1,), (1,)), ((), ())), preferred_element_type=F32)

        if masked:
            causal = (lax.broadcasted_iota(jnp.int32, (blk, blk), 0)
                      >= lax.broadcasted_iota(jnp.int32, (blk, blk), 1))

        def softmax_part(h, s):
            s = s - f_ref[h, pl.ds(j, 1), :]
            if masked:
                s = jnp.where(causal, s, -jnp.inf)
            m_old = m_ref[h]
            m_new = jnp.maximum(m_old, jnp.max(s, axis=1, keepdims=True))
            m_ref[h] = m_new
            return jnp.exp(s - jnp.tile(m_new, (1, blk // LANES))).astype(BF16), jnp.exp(m_old - m_new)

        def accumulate(h, p, alpha):
            pv = jnp.dot(p, vaug_ref[h, pl.ds(start, blk), :], preferred_element_type=F32)
            acc_ref[h] = jnp.tile(alpha, (1, 2)) * acc_ref[h] + pv

        s_cur = scores(0)
        pending = None
        for h in range(hp):
            s_nxt = scores(h + 1) if h + 1 < hp else None
            p, alpha = softmax_part(h, s_cur)
            if pending is not None:
                accumulate(*pending)
            pending = (h, p, alpha)
            s_cur = s_nxt
        accumulate(*pending)

    def loop_body(j, carry):
        step(j, False)
        return carry

    lax.fori_loop(0, qi, loop_body, 0)
    step(qi, True)
    for h in range(hp):
        acc = acc_ref[h]
        o_ref[:, h * dh:(h + 1) * dh] = (acc[:, 0:dh] / acc[:, dh:2 * dh]).astype(o_ref.dtype)


def _attention(qkv, fcum_t, *, batch, length, heads):
    m = qkv.shape[0]
    dh = FOX_HEAD_DIM
    hp = ATT_HEADS_PER_STEP
    assert dh == LANES and heads % hp == 0
    blk = _tile(length, ATT_BLOCK)
    nq = length // blk
    ng = heads // hp
    fcum_t = fcum_t.reshape(batch * ng, hp, nq, blk)
    return pl.pallas_call(
        functools.partial(_attn_body, blk=blk, hp=hp),
        grid=(batch, ng, nq),
        in_specs=[
            pl.BlockSpec((blk, hp * dh), lambda b, g, i: (b * nq + i, g)),
            pl.BlockSpec((length, hp * dh), lambda b, g, i: (b, ng + g)),
            pl.BlockSpec((length, hp * dh), lambda b, g, i: (b, 2 * ng + g)),
            pl.BlockSpec((None, hp, nq, blk), lambda b, g, i: (b * ng + g, 0, 0, 0)),
        ],
        out_specs=pl.BlockSpec((blk, hp * dh), lambda b, g, i: (b * nq + i, g)),
        out_shape=jax.ShapeDtypeStruct((m, heads * dh), BF16),
        scratch_shapes=[pltpu.VMEM((hp, length, 2 * dh), BF16),
                        pltpu.VMEM((hp, blk, LANES), F32),
                        pltpu.VMEM((hp, blk, 2 * dh), F32)],
        compiler_params=_cparams("parallel", "parallel", "arbitrary"),
        name="fox_attention",
    )(qkv, qkv, qkv, fcum_t)


def _merge_body(ys_ref, ya_ref, ws_ref, wa_ref, gs_ref, ga_ref, o_ref):
    ps = jnp.dot(ys_ref[...], ws_ref[...], preferred_element_type=F32)
    pa = jnp.dot(ya_ref[...], wa_ref[...], preferred_element_type=F32)
    o_ref[...] = (gs_ref[...].astype(F32) * ps + ga_ref[...].astype(F32) * pa).astype(o_ref.dtype)


def _merge(y_ssd, y_att, w_ssd, w_att, gates, *, tm=512, tn=256):
    m, ks = y_ssd.shape
    ka = y_att.shape[1]
    n = w_ssd.shape[1]
    tm, tn = _tile(m, tm), _tile(n, tn)
    nj = n // tn
    return pl.pallas_call(
        _merge_body,
        grid=(m // tm, nj),
        in_specs=[
            pl.BlockSpec((tm, ks), lambda i, j: (i, 0)),
            pl.BlockSpec((tm, ka), lambda i, j: (i, 0)),
            pl.BlockSpec((ks, tn), lambda i, j: (0, j)),
            pl.BlockSpec((ka, tn), lambda i, j: (0, j)),
            pl.BlockSpec((tm, tn), lambda i, j: (i, j)),
            pl.BlockSpec((tm, tn), lambda i, j: (i, nj + j)),
        ],
        out_specs=pl.BlockSpec((tm, tn), lambda i, j: (i, j)),
        out_shape=jax.ShapeDtypeStruct((m, n), BF16),
        compiler_params=_cparams("parallel", "arbitrary"),
        name="merge_proj",
    )(y_ssd, y_att, w_ssd, w_att, gates, gates)


def _mm_residual_body(a_ref, w_ref, r_ref, o_ref, *, alpha):
    acc = jnp.dot(a_ref[...], w_ref[...], preferred_element_type=F32)
    o_ref[...] = alpha * r_ref[...] + acc


def _matmul_residual(a, w, resid, alpha, *, tm, tn, name):
    m, k = a.shape
    n = w.shape[1]
    tm, tn = _tile(m, tm), _tile(n, tn)
    return pl.pallas_call(
        functools.partial(_mm_residual_body, alpha=alpha),
        grid=(m // tm, n // tn),
        in_specs=[
            pl.BlockSpec((tm, k), lambda i, j: (i, 0)),
            pl.BlockSpec((k, tn), lambda i, j: (0, j)),
            pl.BlockSpec((tm, tn), lambda i, j: (i, j)),
        ],
        out_specs=pl.BlockSpec((tm, tn), lambda i, j: (i, j)),
        out_shape=jax.ShapeDtypeStruct((m, n), F32),
        compiler_params=_cparams("parallel", "arbitrary"),
        name=name,
    )(a, w, resid)


def _ln_body(x_ref, g_ref, b_ref, *o_refs):
    x = x_ref[...]
    mu = jnp.mean(x, axis=-1, keepdims=True)
    xc = x - mu
    var = jnp.mean(xc * xc, axis=-1, keepdims=True)
    out = xc * lax.rsqrt(var + LN_EPS) * g_ref[...] + b_ref[...]
    for o_ref in o_refs:
        o_ref[...] = out.astype(o_ref.dtype)


def _layer_norm(x, gain, bias, out_dtypes, *, tm=256, name):
    m, d = x.shape
    tm = _tile(m, tm, SUBLANES)
    outs = pl.pallas_call(
        _ln_body,
        grid=(m // tm,),
        in_specs=[pl.BlockSpec((tm, d), lambda i: (i, 0)),
                  pl.BlockSpec((1, d), lambda i: (0, 0)),
                  pl.BlockSpec((1, d), lambda i: (0, 0))],
        out_specs=[pl.BlockSpec((tm, d), lambda i: (i, 0)) for _ in out_dtypes],
        out_shape=[jax.ShapeDtypeStruct((m, d), dt) for dt in out_dtypes],
        compiler_params=_cparams("parallel"),
        name=name,
    )(x, gain, bias)
    return outs


def _ffn_up_body(a_ref, halo_ref, wv_ref, wg_ref, cwv_ref, cwg_ref, cbv_ref, cbg_ref, o_ref,
                 wbf_ref, *, kw, tiles_per_seq):
    i = pl.program_id(1)
    pad = BF16_SUBLANES

    @pl.when(i == 0)
    def _():
        wbf_ref[0] = wv_ref[...].astype(BF16)
        wbf_ref[1] = wg_ref[...].astype(BF16)

    a = a_ref[...]
    halo = halo_ref[...]
    keep = (i % tiles_per_seq != 0).astype(F32)

    def conv(c, cw_ref, cb_ref):
        w = wbf_ref[c]
        u = jnp.dot(a, w, preferred_element_type=F32)
        uh = jnp.dot(halo, w, preferred_element_type=F32) * keep
        cw = cw_ref[...]
        cb = cb_ref[...]
        out = cb + cw[kw - 1:kw, :] * u
        head_src = jnp.concatenate([uh, u[0:pad, :]], axis=0)
        head = cb + cw[kw - 1:kw, :] * head_src[pad:2 * pad, :]
        for k in range(kw - 1):
            shift = kw - 1 - k
            out = out + cw[k:k + 1, :] * pltpu.roll(u, shift, axis=0)
            head = head + cw[k:k + 1, :] * head_src[pad - shift:2 * pad - shift, :]
        return out, head

    gate, gate_head = conv(1, cwg_ref, cbg_ref)
    gate, gate_head = _silu(gate), _silu(gate_head)
    val, val_head = conv(0, cwv_ref, cbv_ref)
    o_ref[...] = (gate * val).astype(o_ref.dtype)
    o_ref[0:pad, :] = (gate_head * val_head).astype(o_ref.dtype)


def _ffn_up(h, w_up, conv_w, conv_b, *, length, d_ff, tm=1024, tn=256):
    m, k = h.shape
    tm = _tile(min(m, length), tm, BF16_SUBLANES)
    assert length % tm == 0
    tn = _tile(d_ff, tn)
    nj = d_ff // tn
    kw = conv_w.shape[0]
    hb = tm // BF16_SUBLANES
    return pl.pallas_call(
        functools.partial(_ffn_up_body, kw=kw, tiles_per_seq=length // tm),
        grid=(nj, m // tm),
        in_specs=[
            pl.BlockSpec((tm, k), lambda j, i: (i, 0)),
            pl.BlockSpec((BF16_SUBLANES, k), lambda j, i: (jnp.maximum(i * hb - 1, 0), 0)),
            pl.BlockSpec((k, tn), lambda j, i: (0, j)),
            pl.BlockSpec((k, tn), lambda j, i: (0, nj + j)),
            pl.BlockSpec((kw, tn), lambda j, i: (0, j)),
            pl.BlockSpec((kw, tn), lambda j, i: (0, nj + j)),
            pl.BlockSpec((1, tn), lambda j, i: (0, j)),
            pl.BlockSpec((1, tn), lambda j, i: (0, nj + j)),
        ],
        out_specs=pl.BlockSpec((tm, tn), lambda j, i: (i, j)),
        out_shape=jax.ShapeDtypeStruct((m, d_ff), BF16),
        scratch_shapes=[pltpu.VMEM((2, k, tn), BF16)],
        compiler_params=_cparams("parallel", "arbitrary"),
        name="ffn_up_conv_act",
    )(h, h, w_up, w_up, conv_w, conv_w, conv_b, conv_b)


def _layer(h, p, *, batch, length, alpha):
    m, d = h.shape
    d_inner = p["ssd_norm_w"].shape[-1]
    conv_dim = p["ssd_conv_b"].shape[-1]
    ssd_heads = p["ssd_dt_bias"].shape[-1]
    fox_heads = p["fox_f_bias"].shape[-1]
    d_att = fox_heads * FOX_HEAD_DIM
    d_ff = p["w_down"].shape[0]
    groups = (conv_dim - d_inner) // (2 * SSD_STATE)
    assert ssd_heads <= LANES and fox_heads <= LANES

    o_z, o_xbc = 0, d_inner
    o_dt = o_xbc + conv_dim
    o_q = o_dt + ssd_heads
    o_f = o_q + 3 * d_att
    o_g = o_f + fox_heads
    wt_in = p["w_in"].T.astype(BF16)
    zeros = lambda n: jnp.zeros((n, d), BF16)
    wt_small = jnp.concatenate([wt_in[o_dt:o_q], zeros(LANES - ssd_heads),
                                wt_in[o_f:o_g], zeros(LANES - fox_heads)], axis=0)

    h_bf = h.astype(BF16)
    zx = _proj(h_bf, wt_in, F32, row0=o_z, n=o_dt, name="in_proj_zx")
    qkv = _proj(h_bf, wt_in, BF16, row0=o_q, n=3 * d_att, scaled_cols=d_att, scale=1.0 / math.sqrt(FOX_HEAD_DIM),
                name="in_proj_qkv")
    gates = _proj(h_bf, wt_in, F32, row0=o_g, n=2 * d, bias=p["gate_bias"].reshape(1, 2 * d), name="in_proj_gates")
    small = _proj(h_bf, wt_small, F32, name="in_proj_small")

    pad_row = lambda v, n: jnp.pad(v.reshape(1, -1).astype(F32), ((0, 0), (0, n - v.shape[-1])))
    y_ssd = _ssd(zx, small, p["ssd_conv_w"], p["ssd_conv_b"].reshape(1, -1),
                 pad_row(p["ssd_dt_bias"], LANES), pad_row(p["ssd_a_log"], LANES),
                 jnp.repeat(p["ssd_d"].astype(F32), SSD_HEAD_DIM).reshape(1, -1),
                 p["ssd_norm_w"].reshape(1, -1),
                 batch=batch, length=length, d_inner=d_inner, groups=groups)

    fcum = _fcum(small, pad_row(p["fox_f_bias"], LANES), batch=batch, length=length, heads=fox_heads)
    y_att = _attention(qkv, fcum, batch=batch, length=length, heads=fox_heads)

    merged = _merge(y_ssd, y_att, p["w_proj_ssd"].astype(BF16), p["w_proj_att"].astype(BF16), gates)
    pre1 = _matmul_residual(merged, p["w_out"].astype(BF16), h, alpha, tm=1024, tn=512, name="out_proj_residual")
    h1, h1_bf = _layer_norm(pre1, p["ln1_g"].reshape(1, -1), p["ln1_b"].reshape(1, -1), (F32, BF16), name="layer_norm_1")

    act = _ffn_up(h1_bf, p["w_up"], p["ffn_conv_w"], p["ffn_conv_b"].reshape(1, -1),
                  length=length, d_ff=d_ff)
    pre2 = _matmul_residual(act, p["w_down"].astype(BF16), h1, alpha, tm=512, tn=256, name="ffn_down_residual")
    (out,) = _layer_norm(pre2, p["ln2_g"].reshape(1, -1), p["ln2_b"].reshape(1, -1), (F32,), name="layer_norm_2")
    return out


_PARAM_NAMES = ("w_in", "ssd_conv_w", "ssd_conv_b", "ssd_dt_bias", "ssd_a_log", "ssd_d", "ssd_norm_w",
                "fox_f_bias", "gate_bias", "w_proj_ssd", "w_proj_att", "w_out", "ln1_g", "ln1_b",
                "w_up", "ffn_conv_w", "ffn_conv_b", "w_down", "ln2_g", "ln2_b")


def kernel(x, w_in, ssd_conv_w, ssd_conv_b, ssd_dt_bias, ssd_a_log, ssd_d, ssd_norm_w, fox_f_bias, gate_bias,
           w_proj_ssd, w_proj_att, w_out, ln1_g, ln1_b, w_up, ffn_conv_w, ffn_conv_b, w_down, ln2_g, ln2_b):
    params = (w_in, ssd_conv_w, ssd_conv_b, ssd_dt_bias, ssd_a_log, ssd_d, ssd_norm_w, fox_f_bias, gate_bias,
              w_proj_ssd, w_proj_att, w_out, ln1_g, ln1_b, w_up, ffn_conv_w, ffn_conv_b, w_down, ln2_g, ln2_b)
    batch, length, d = x.shape
    depth = w_in.shape[0]
    alpha = (2.0 * depth) ** 0.25
    h = x.reshape(batch * length, d)
    for layer in range(depth):
        p = {name: arr[layer] for name, arr in zip(_PARAM_NAMES, params)}
        h = _layer(h, p, batch=batch, length=length, alpha=alpha)
    return h.reshape(batch, length, d)
```

```python
import functools
import math

import jax
import jax.numpy as jnp
from jax import lax
from jax.experimental import pallas as pl
from jax.experimental.pallas import tpu as pltpu

F32 = jnp.float32
BF16 = jnp.bfloat16

SSD_HEAD_DIM = 64
SSD_STATE = 128
FOX_HEAD_DIM = 128
LN_EPS = 1e-5
RMS_EPS = 1e-5

LANES = 128
SUBLANES = 8
BF16_SUBLANES = 16
VMEM_LIMIT_BYTES = 56 * 1024 * 1024

SSD_CHUNK = 128
ATT_BLOCK = 512
ATT_HEADS_PER_STEP = 4
CUM_BLOCK = 128
FFN_ROW_SPLIT = 1


def _cparams(*sem):
    return pltpu.CompilerParams(dimension_semantics=sem, vmem_limit_bytes=VMEM_LIMIT_BYTES)


def _tile(n, pref, quantum=LANES):
    if n <= pref:
        return n
    t = (pref // quantum) * quantum
    while t > quantum and n % t:
        t -= quantum
    assert n % t == 0, (n, pref, quantum)
    return t


def _softplus(x):
    return jnp.maximum(x, 0.0) + jnp.log1p(jnp.exp(-jnp.abs(x)))


def _log_sigmoid(x):
    return jnp.minimum(x, 0.0) - jnp.log1p(jnp.exp(-jnp.abs(x)))


def _silu(x):
    return x * jax.nn.sigmoid(x)


def _proj_body(a_ref, wt_ref, *rest, scaled_tiles, scale, sigmoid):
    if sigmoid:
        b_ref, o_ref = rest
    else:
        (o_ref,) = rest
    acc = lax.dot_general(a_ref[...], wt_ref[...], (((1,), (1,)), ((), ())), preferred_element_type=F32)
    if scaled_tiles:
        acc = acc * jnp.where(pl.program_id(1) < scaled_tiles, scale, 1.0)
    if sigmoid:
        acc = jax.nn.sigmoid(acc + b_ref[...])
    o_ref[...] = acc.astype(o_ref.dtype)


def _proj(a, wt, out_dtype, *, row0=0, n=None, bias=None, scaled_cols=0, scale=1.0, tm=1024, tn=1024, name):
    m, k = a.shape
    n = wt.shape[0] if n is None else n
    tm, tn = _tile(m, tm), _tile(n, tn)
    assert scaled_cols % tn == 0 and row0 % BF16_SUBLANES == 0
    in_specs = [pl.BlockSpec((tm, k), lambda i, j: (i, 0)),
                pl.BlockSpec((pl.Element(tn), pl.Element(k)),
                             lambda i, j: (pl.multiple_of(row0 + j * tn, BF16_SUBLANES), 0))]
    args = [a, wt]
    if bias is not None:
        in_specs.append(pl.BlockSpec((1, tn), lambda i, j: (0, j)))
        args.append(bias)
    return pl.pallas_call(
        functools.partial(_proj_body, scaled_tiles=scaled_cols // tn, scale=scale, sigmoid=bias is not None),
        grid=(m // tm, n // tn),
        in_specs=in_specs,
        out_specs=pl.BlockSpec((tm, tn), lambda i, j: (i, j)),
        out_shape=jax.ShapeDtypeStruct((m, n), out_dtype),
        compiler_params=_cparams("parallel", "arbitrary"),
        name=name,
    )(*args)


def _ssd_prep_body(dt_ref, dtb_ref, alog_ref, acs_ref, dtt_ref, acst_ref):
    q = dt_ref.shape[0]
    dtv = _softplus(dt_ref[...] + dtb_ref[...])
    da = dtv * (-jnp.exp(alog_ref[...]))
    row = lax.broadcasted_iota(jnp.int32, (q, q), 0)
    col = lax.broadcasted_iota(jnp.int32, (q, q), 1)
    acs = jnp.dot((row >= col).astype(F32), da, preferred_element_type=F32, precision=lax.Precision.HIGHEST)
    acs_ref[...] = acs
    dtt_ref[...] = dtv.T
    acst_ref[...] = acs.T


def _ssd_prep(small, dt_bias, a_log):
    m = small.shape[0]
    q = SSD_CHUNK
    nat = pl.BlockSpec((q, LANES), lambda i: (i, 0))
    tr = pl.BlockSpec((LANES, q), lambda i: (0, i))
    vec = pl.BlockSpec((1, LANES), lambda i: (0, 0))
    return pl.pallas_call(
        _ssd_prep_body,
        grid=(m // q,),
        in_specs=[nat, vec, vec],
        out_specs=[nat, tr, tr],
        out_shape=[jax.ShapeDtypeStruct((m, LANES), F32), jax.ShapeDtypeStruct((LANES, m), F32),
                   jax.ShapeDtypeStruct((LANES, m), F32)],
        compiler_params=_cparams("parallel"),
        name="ssd_dt_prep",
    )(small, dt_bias, a_log)


def _ssd_body(z_ref, xs_ref, b_ref, c_ref, acs_ref, dtt_ref, acst_ref, px_ref, pb_ref, pc_ref, y_ref,
              h_ref, tail_ref, ybuf_ref, *, q, r, kw):
    g = pl.program_id(1)
    c = pl.program_id(2)
    gw = r * SSD_HEAD_DIM
    n = SSD_STATE

    @pl.when(c == 0)
    def _():
        h_ref[...] = jnp.zeros_like(h_ref)
        tail_ref[...] = jnp.zeros_like(tail_ref)

    def conv_silu(cur_ref, lo, hi, p_ref):
        cur = cur_ref[...].astype(F32)
        p = p_ref[...]
        bias = p[kw:kw + 1, :]
        out = bias + p[kw - 1:kw, :] * cur
        head_src = jnp.concatenate([tail_ref[:, lo:hi], cur[0:SUBLANES, :]], axis=0)
        head = bias + p[kw - 1:kw, :] * head_src[SUBLANES:2 * SUBLANES, :]
        for k in range(kw - 1):
            shift = kw - 1 - k
            out = out + p[k:k + 1, :] * pltpu.roll(cur, shift, axis=0)
            head = head + p[k:k + 1, :] * head_src[SUBLANES - shift:2 * SUBLANES - shift, :]
        tail_ref[:, lo:hi] = cur[q - SUBLANES:q, :]
        return _silu(jnp.concatenate([head, out[SUBLANES:, :]], axis=0))

    xs = conv_silu(xs_ref, 0, gw, px_ref)
    bm = conv_silu(b_ref, gw, gw + n, pb_ref)
    cm = conv_silu(c_ref, gw + n, gw + 2 * n, pc_ref)

    row = lax.broadcasted_iota(jnp.int32, (q, q), 0)
    col = lax.broadcasted_iota(jnp.int32, (q, q), 1)
    tri = row >= col
    acs_g = pltpu.roll(acs_ref[...], (LANES - g * r) % LANES, axis=1)
    dt_t = dtt_ref[...]
    acs_t = acst_ref[...]

    cm_bf = cm.astype(BF16)
    cb = lax.dot_general(cm_bf, bm.astype(BF16), (((1,), (1,)), ((), ())), preferred_element_type=F32)
    bm_t = bm.T

    lane = lax.broadcasted_iota(jnp.int32, (q, LANES), 1)
    lo_half = lane < SSD_HEAD_DIM
    lo_half_row = lo_half[0:1, :]
    pairs = range(r // 2)
    pair_cols = [slice(j * LANES, (j + 1) * LANES) for j in pairs]
    y_off = jnp.dot(cm_bf, h_ref[...].astype(BF16), preferred_element_type=F32)

    lhs_y, lhs_s, rhs, e_pair, cd_pair = [], [], [], [], []
    for j in pairs:
        xs_p = xs[:, pair_cols[j]]
        rhs.append(jnp.concatenate([jnp.where(lo_half, xs_p, 0.0).astype(BF16),
                                    jnp.where(lo_half, 0.0, xs_p).astype(BF16)], axis=0))
        m_parts, bw_parts, e_cols, cd = [], [], [], []
        for hd in (2 * j, 2 * j + 1):
            a_col = acs_g[:, hd:hd + 1]
            a_row = acs_t[hd:hd + 1, :]
            dt_row = dt_t[hd:hd + 1, :]
            a_last = acs_g[q - 1:q, hd:hd + 1]
            lmat = jnp.exp(jnp.where(tri, a_col - a_row, -jnp.inf))
            m_parts.append(cb * lmat * dt_row)
            bw_parts.append(bm_t * (jnp.exp(a_last - a_row) * dt_row))
            e_cols.append(jnp.exp(a_col))
            cd.append(jnp.exp(a_last))
        lhs_y.append(jnp.concatenate(m_parts, axis=1).astype(BF16))
        lhs_s.append(jnp.concatenate(bw_parts, axis=1).astype(BF16))
        e_pair.append(jnp.where(lo_half, e_cols[0], e_cols[1]))
        cd_pair.append(jnp.where(lo_half_row, cd[0], cd[1]))
    y_diag = [jnp.dot(lhs_y[j], rhs[j], preferred_element_type=F32) for j in pairs]
    s_new = [jnp.dot(lhs_s[j], rhs[j], preferred_element_type=F32) for j in pairs]
    ssq = jnp.zeros((q, 1), F32)
    for j in pairs:
        cols = pair_cols[j]
        h_ref[:, cols] = cd_pair[j] * h_ref[:, cols] + s_new[j]
        y = y_diag[j] + e_pair[j] * y_off[:, cols] + px_ref[kw + 1:kw + 2, cols] * xs[:, cols]
        y = y * _silu(z_ref[:, cols].astype(F32))
        ybuf_ref[:, cols] = y
        ssq = ssq + jnp.sum(y * y, axis=1, keepdims=True)
    inv = lax.rsqrt(ssq / gw + RMS_EPS)
    y_ref[...] = (ybuf_ref[...] * inv * px_ref[kw + 2:kw + 3, :]).astype(y_ref.dtype)


def _ssd(zx, small, conv_w, conv_b, dt_bias, a_log, d_cols, norm_w, *, batch, length, d_inner, groups):
    m = zx.shape[0]
    q = SSD_CHUNK
    assert length % q == 0
    nc = length // q
    gw = d_inner // groups
    r = gw // SSD_HEAD_DIM
    assert r % 2 == 0 and r % SUBLANES == 0 and r * groups <= LANES
    kw = conv_w.shape[0]
    n = SSD_STATE
    acs, dt_t, acs_t = _ssd_prep(small, dt_bias, a_log)
    conv_dim = conv_w.shape[1]
    rows = jnp.concatenate([conv_w, conv_b], axis=0)
    extra = jnp.zeros((2, conv_dim), F32).at[0, :d_inner].set(d_cols[0]).at[1, :d_inner].set(norm_w[0])
    params = jnp.concatenate([rows, extra], axis=0)
    np_rows = params.shape[0]
    zblk = d_inner // gw
    bblk = 2 * d_inner // n
    cblk_w = d_inner // n
    row = lambda b, g, c: b * nc + c
    in_specs = [
        pl.BlockSpec((q, gw), lambda b, g, c: (row(b, g, c), g)),
        pl.BlockSpec((q, gw), lambda b, g, c: (row(b, g, c), zblk + g)),
        pl.BlockSpec((q, n), lambda b, g, c: (row(b, g, c), bblk + g)),
        pl.BlockSpec((q, n), lambda b, g, c: (row(b, g, c), bblk + groups + g)),
        pl.BlockSpec((q, LANES), lambda b, g, c: (row(b, g, c), 0)),
        pl.BlockSpec((r, q), lambda b, g, c: (g, row(b, g, c))),
        pl.BlockSpec((r, q), lambda b, g, c: (g, row(b, g, c))),
        pl.BlockSpec((np_rows, gw), lambda b, g, c: (0, g)),
        pl.BlockSpec((np_rows, n), lambda b, g, c: (0, cblk_w + g)),
        pl.BlockSpec((np_rows, n), lambda b, g, c: (0, cblk_w + groups + g)),
    ]
    return pl.pallas_call(
        functools.partial(_ssd_body, q=q, r=r, kw=kw),
        grid=(batch, groups, nc),
        in_specs=in_specs,
        out_specs=pl.BlockSpec((q, gw), lambda b, g, c: (row(b, g, c), g)),
        out_shape=jax.ShapeDtypeStruct((m, d_inner), BF16),
        scratch_shapes=[
            pltpu.VMEM((n, gw), F32),
            pltpu.VMEM((SUBLANES, gw + 2 * n), F32),
            pltpu.VMEM((q, gw), F32),
        ],
        compiler_params=_cparams("parallel", "parallel", "arbitrary"),
        name="ssd_scan",
    )(zx, zx, zx, zx, acs, dt_t, acs_t, params, params, params)


def _fcum_body(f_ref, bias_ref, o_ref, carry_ref, *, heads):
    @pl.when(pl.program_id(1) == 0)
    def _():
        carry_ref[...] = jnp.zeros_like(carry_ref)

    t = f_ref.shape[0]
    logf = _log_sigmoid(f_ref[...] + bias_ref[...])
    row = lax.broadcasted_iota(jnp.int32, (t, t), 0)
    col = lax.broadcasted_iota(jnp.int32, (t, t), 1)
    cs = jnp.dot((row >= col).astype(F32), logf, preferred_element_type=F32,
                 precision=lax.Precision.HIGHEST) + carry_ref[...]
    carry_ref[...] = cs[t - 1:t, :]
    o_ref[...] = cs.T[0:heads, :]


def _fcum(small, f_bias, *, batch, length, heads):
    t = CUM_BLOCK
    nc = length // t
    return pl.pallas_call(
        functools.partial(_fcum_body, heads=heads),
        grid=(batch, nc),
        in_specs=[pl.BlockSpec((t, LANES), lambda b, c: (b * nc + c, 1)),
                  pl.BlockSpec((1, LANES), lambda b, c: (0, 0))],
        out_specs=pl.BlockSpec((None, heads, t), lambda b, c: (b, 0, c)),
        out_shape=jax.ShapeDtypeStruct((batch, heads, length), F32),
        scratch_shapes=[pltpu.VMEM((1, LANES), F32)],
        compiler_params=_cparams("parallel", "arbitrary"),
        name="fox_fcum",
    )(small, f_bias)


def _attn_body(q_ref, k_ref, v_ref, f_ref, o_ref, vaug_ref, m_ref, acc_ref, *, blk, hp):
    qi = pl.program_id(2)
    dh = FOX_HEAD_DIM
    length = k_ref.shape[0]

    @pl.when(qi == 0)
    def _():
        for h in range(hp):
            vaug_ref[h, :, 0:dh] = v_ref[:, h * dh:(h + 1) * dh]
            vaug_ref[h, :, dh:2 * dh] = jnp.ones((length, dh), BF16)

    m_ref[...] = jnp.full_like(m_ref, -jnp.inf)
    acc_ref[...] = jnp.zeros_like(acc_ref)

    def step(j, masked):
        start = pl.multiple_of(j * blk, blk)

        def scores(h):
            qv = q_ref[:, h * dh:(h + 1) * dh]
            kj = k_ref[pl.ds(start, blk), h * dh:(h + 1) * dh]
            return lax.dot_general(qv, kj, (((1,), (1,)), ((), ())), preferred_element_type=F32)

        if masked:
            causal = (lax.broadcasted_iota(jnp.int32, (blk, blk), 0)
                      >= lax.broadcasted_iota(jnp.int32, (blk, blk), 1))

        def softmax_part(h, s):
            s = s - f_ref[h, pl.ds(j, 1), :]
            if masked:
                s = jnp.where(causal, s, -jnp.inf)
            m_old = m_ref[h]
            m_new = jnp.maximum(m_old, jnp.max(s, axis=1, keepdims=True))
            m_ref[h] = m_new
            return jnp.exp(s - jnp.tile(m_new, (1, blk // LANES))).astype(BF16), jnp.exp(m_old - m_new)

        def accumulate(h, p, alpha):
            pv = jnp.dot(p, vaug_ref[h, pl.ds(start, blk), :], preferred_element_type=F32)
            acc_ref[h] = jnp.tile(alpha, (1, 2)) * acc_ref[h] + pv

        s_cur = scores(0)
        pending = None
        for h in range(hp):
            s_nxt = scores(h + 1) if h + 1 < hp else None
            p, alpha = softmax_part(h, s_cur)
            if pending is not None:
                accumulate(*pending)
            pending = (h, p, alpha)
            s_cur = s_nxt
        accumulate(*pending)

    def loop_body(j, carry):
        step(j, False)
        return carry

    lax.fori_loop(0, qi, loop_body, 0)
    step(qi, True)
    for h in range(hp):
        acc = acc_ref[h]
        o_ref[:, h * dh:(h + 1) * dh] = (acc[:, 0:dh] / acc[:, dh:2 * dh]).astype(o_ref.dtype)


def _attention(qkv, fcum_t, *, batch, length, heads):
    m = qkv.shape[0]
    dh = FOX_HEAD_DIM
    hp = ATT_HEADS_PER_STEP
    assert dh == LANES and heads % hp == 0
    blk = _tile(length, ATT_BLOCK)
    nq = length // blk
    ng = heads // hp
    fcum_t = fcum_t.reshape(batch * ng, hp, nq, blk)
    return pl.pallas_call(
        functools.partial(_attn_body, blk=blk, hp=hp),
        grid=(batch, ng, nq),
        in_specs=[
            pl.BlockSpec((blk, hp * dh), lambda b, g, i: (b * nq + i, g)),
            pl.BlockSpec((length, hp * dh), lambda b, g, i: (b, ng + g)),
            pl.BlockSpec((length, hp * dh), lambda b, g, i: (b, 2 * ng + g)),
            pl.BlockSpec((None, hp, nq, blk), lambda b, g, i: (b * ng + g, 0, 0, 0)),
        ],
        out_specs=pl.BlockSpec((blk, hp * dh), lambda b, g, i: (b * nq + i, g)),
        out_shape=jax.ShapeDtypeStruct((m, heads * dh), BF16),
        scratch_shapes=[pltpu.VMEM((hp, length, 2 * dh), BF16),
                        pltpu.VMEM((hp, blk, LANES), F32),
                        pltpu.VMEM((hp, blk, 2 * dh), F32)],
        compiler_params=_cparams("parallel", "parallel", "arbitrary"),
        name="fox_attention",
    )(qkv, qkv, qkv, fcum_t)


def _merge_body(ys_ref, ya_ref, ws_ref, wa_ref, gs_ref, ga_ref, o_ref):
    ps = jnp.dot(ys_ref[...], ws_ref[...], preferred_element_type=F32)
    pa = jnp.dot(ya_ref[...], wa_ref[...], preferred_element_type=F32)
    o_ref[...] = (gs_ref[...].astype(F32) * ps + ga_ref[...].astype(F32) * pa).astype(o_ref.dtype)


def _merge(y_ssd, y_att, w_ssd, w_att, gates, *, tm=1024, tn=256):
    m, ks = y_ssd.shape
    ka = y_att.shape[1]
    n = w_ssd.shape[1]
    tm, tn = _tile(m, tm), _tile(n, tn)
    nj = n // tn
    once = pl.Buffered(1)
    return pl.pallas_call(
        _merge_body,
        grid=(m // tm, nj),
        in_specs=[
            pl.BlockSpec((tm, ks), lambda i, j: (i, 0), pipeline_mode=once),
            pl.BlockSpec((tm, ka), lambda i, j: (i, 0), pipeline_mode=once),
            pl.BlockSpec((ks, tn), lambda i, j: (0, j)),
            pl.BlockSpec((ka, tn), lambda i, j: (0, j)),
            pl.BlockSpec((tm, tn), lambda i, j: (i, j)),
            pl.BlockSpec((tm, tn), lambda i, j: (i, nj + j)),
        ],
        out_specs=pl.BlockSpec((tm, tn), lambda i, j: (i, j)),
        out_shape=jax.ShapeDtypeStruct((m, n), BF16),
        compiler_params=_cparams("parallel", "arbitrary"),
        name="merge_proj",
    )(y_ssd, y_att, w_ssd, w_att, gates, gates)


def _mm_residual_body(a_ref, w_ref, r_ref, o_ref, *, alpha):
    acc = jnp.dot(a_ref[...], w_ref[...], preferred_element_type=F32)
    o_ref[...] = alpha * r_ref[...] + acc


def _matmul_residual(a, w, resid, alpha, *, tm, tn, lhs_buffers=2, name):
    m, k = a.shape
    n = w.shape[1]
    tm, tn = _tile(m, tm), _tile(n, tn)
    return pl.pallas_call(
        functools.partial(_mm_residual_body, alpha=alpha),
        grid=(m // tm, n // tn),
        in_specs=[
            pl.BlockSpec((tm, k), lambda i, j: (i, 0), pipeline_mode=pl.Buffered(lhs_buffers)),
            pl.BlockSpec((k, tn), lambda i, j: (0, j)),
            pl.BlockSpec((tm, tn), lambda i, j: (i, j)),
        ],
        out_specs=pl.BlockSpec((tm, tn), lambda i, j: (i, j)),
        out_shape=jax.ShapeDtypeStruct((m, n), F32),
        compiler_params=_cparams("parallel", "arbitrary"),
        name=name,
    )(a, w, resid)


def _ln_body(x_ref, g_ref, b_ref, *o_refs):
    x = x_ref[...]
    mu = jnp.mean(x, axis=-1, keepdims=True)
    xc = x - mu
    var = jnp.mean(xc * xc, axis=-1, keepdims=True)
    out = xc * lax.rsqrt(var + LN_EPS) * g_ref[...] + b_ref[...]
    for o_ref in o_refs:
        o_ref[...] = out.astype(o_ref.dtype)


def _layer_norm(x, gain, bias, out_dtypes, *, tm=256, name):
    m, d = x.shape
    tm = _tile(m, tm, SUBLANES)
    outs = pl.pallas_call(
        _ln_body,
        grid=(m // tm,),
        in_specs=[pl.BlockSpec((tm, d), lambda i: (i, 0)),
                  pl.BlockSpec((1, d), lambda i: (0, 0)),
                  pl.BlockSpec((1, d), lambda i: (0, 0))],
        out_specs=[pl.BlockSpec((tm, d), lambda i: (i, 0)) for _ in out_dtypes],
        out_shape=[jax.ShapeDtypeStruct((m, d), dt) for dt in out_dtypes],
        compiler_params=_cparams("parallel"),
        name=name,
    )(x, gain, bias)
    return outs


def _ffn_up_body(a_ref, halo_ref, wv_ref, wg_ref, cwv_ref, cwg_ref, cbv_ref, cbg_ref, o_ref,
                 wbf_ref, *, kw, tiles_per_seq):
    i = pl.program_id(1)
    pad = BF16_SUBLANES

    @pl.when(i == 0)
    def _():
        wbf_ref[0] = wv_ref[...].astype(BF16)
        wbf_ref[1] = wg_ref[...].astype(BF16)

    halo = halo_ref[...]
    keep = (i % tiles_per_seq != 0).astype(F32)
    tm = a_ref.shape[0]
    rows = tm // FFN_ROW_SPLIT
    history = [None, None]

    def conv(c, a, cw_ref, cb_ref, first):
        w = wbf_ref[c]
        u = jnp.dot(a, w, preferred_element_type=F32)
        uh = jnp.dot(halo, w, preferred_element_type=F32) * keep if first else history[c]
        history[c] = u[rows - pad:rows, :]
        cw = cw_ref[...]
        cb = cb_ref[...]
        out = cb + cw[kw - 1:kw, :] * u
        head_src = jnp.concatenate([uh, u[0:pad, :]], axis=0)
        head = cb + cw[kw - 1:kw, :] * head_src[pad:2 * pad, :]
        for k in range(kw - 1):
            shift = kw - 1 - k
            out = out + cw[k:k + 1, :] * pltpu.roll(u, shift, axis=0)
            head = head + cw[k:k + 1, :] * head_src[pad - shift:2 * pad - shift, :]
        return out, head

    for piece in range(FFN_ROW_SPLIT):
        r0 = piece * rows
        a = a_ref[r0:r0 + rows, :]
        gate, gate_head = conv(1, a, cwg_ref, cbg_ref, piece == 0)
        gate, gate_head = _silu(gate), _silu(gate_head)
        val, val_head = conv(0, a, cwv_ref, cbv_ref, piece == 0)
        o_ref[r0:r0 + rows, :] = (gate * val).astype(o_ref.dtype)
        o_ref[r0:r0 + pad, :] = (gate_head * val_head).astype(o_ref.dtype)


def _ffn_up(h, w_up, conv_w, conv_b, *, length, d_ff, tm=1024, tn=256):
    m, k = h.shape
    tm = _tile(min(m, length), tm, BF16_SUBLANES)
    assert length % tm == 0
    tn = _tile(d_ff, tn)
    nj = d_ff // tn
    kw = conv_w.shape[0]
    hb = tm // BF16_SUBLANES
    return pl.pallas_call(
        functools.partial(_ffn_up_body, kw=kw, tiles_per_seq=length // tm),
        grid=(nj, m // tm),
        in_specs=[
            pl.BlockSpec((tm, k), lambda j, i: (i, 0)),
            pl.BlockSpec((BF16_SUBLANES, k), lambda j, i: (jnp.maximum(i * hb - 1, 0), 0)),
            pl.BlockSpec((k, tn), lambda j, i: (0, j)),
            pl.BlockSpec((k, tn), lambda j, i: (0, nj + j)),
            pl.BlockSpec((kw, tn), lambda j, i: (0, j)),
            pl.BlockSpec((kw, tn), lambda j, i: (0, nj + j)),
            pl.BlockSpec((1, tn), lambda j, i: (0, j)),
            pl.BlockSpec((1, tn), lambda j, i: (0, nj + j)),
        ],
        out_specs=pl.BlockSpec((tm, tn), lambda j, i: (i, j)),
        out_shape=jax.ShapeDtypeStruct((m, d_ff), BF16),
        scratch_shapes=[pltpu.VMEM((2, k, tn), BF16)],
        compiler_params=_cparams("parallel", "arbitrary"),
        name="ffn_up_conv_act",
    )(h, h, w_up, w_up, conv_w, conv_w, conv_b, conv_b)


def _layer(h, p, *, batch, length, alpha):
    m, d = h.shape
    d_inner = p["ssd_norm_w"].shape[-1]
    conv_dim = p["ssd_conv_b"].shape[-1]
    ssd_heads = p["ssd_dt_bias"].shape[-1]
    fox_heads = p["fox_f_bias"].shape[-1]
    d_att = fox_heads * FOX_HEAD_DIM
    d_ff = p["w_down"].shape[0]
    groups = (conv_dim - d_inner) // (2 * SSD_STATE)
    assert ssd_heads <= LANES and fox_heads <= LANES

    o_z, o_xbc = 0, d_inner
    o_dt = o_xbc + conv_dim
    o_q = o_dt + ssd_heads
    o_f = o_q + 3 * d_att
    o_g = o_f + fox_heads
    wt_in = p["w_in"].T.astype(BF16)
    zeros = lambda n: jnp.zeros((n, d), BF16)
    wt_small = jnp.concatenate([wt_in[o_dt:o_q], zeros(LANES - ssd_heads),
                                wt_in[o_f:o_g], zeros(LANES - fox_heads)], axis=0)

    h_bf = h.astype(BF16)
    zx = _proj(h_bf, wt_in, F32, row0=o_z, n=o_dt, name="in_proj_zx")
    qkv = _proj(h_bf, wt_in, BF16, row0=o_q, n=3 * d_att, scaled_cols=d_att, scale=1.0 / math.sqrt(FOX_HEAD_DIM),
                name="in_proj_qkv")
    gates = _proj(h_bf, wt_in, F32, row0=o_g, n=2 * d, bias=p["gate_bias"].reshape(1, 2 * d), name="in_proj_gates")
    small = _proj(h_bf, wt_small, F32, name="in_proj_small")

    pad_row = lambda v, n: jnp.pad(v.reshape(1, -1).astype(F32), ((0, 0), (0, n - v.shape[-1])))
    y_ssd = _ssd(zx, small, p["ssd_conv_w"], p["ssd_conv_b"].reshape(1, -1),
                 pad_row(p["ssd_dt_bias"], LANES), pad_row(p["ssd_a_log"], LANES),
                 jnp.repeat(p["ssd_d"].astype(F32), SSD_HEAD_DIM).reshape(1, -1),
                 p["ssd_norm_w"].reshape(1, -1),
                 batch=batch, length=length, d_inner=d_inner, groups=groups)

    fcum = _fcum(small, pad_row(p["fox_f_bias"], LANES), batch=batch, length=length, heads=fox_heads)
    y_att = _attention(qkv, fcum, batch=batch, length=length, heads=fox_heads)

    merged = _merge(y_ssd, y_att, p["w_proj_ssd"].astype(BF16), p["w_proj_att"].astype(BF16), gates)
    pre1 = _matmul_residual(merged, p["w_out"].astype(BF16), h, alpha, tm=1024, tn=512, name="out_proj_residual")
    h1, h1_bf = _layer_norm(pre1, p["ln1_g"].reshape(1, -1), p["ln1_b"].reshape(1, -1), (F32, BF16), name="layer_norm_1")

    act = _ffn_up(h1_bf, p["w_up"], p["ffn_conv_w"], p["ffn_conv_b"].reshape(1, -1),
                  length=length, d_ff=d_ff)
    pre2 = _matmul_residual(act, p["w_down"].astype(BF16), h1, alpha, tm=1024, tn=256, lhs_buffers=1,
                            name="ffn_down_residual")
    (out,) = _layer_norm(pre2, p["ln2_g"].reshape(1, -1), p["ln2_b"].reshape(1, -1), (F32,), name="layer_norm_2")
    return out


_PARAM_NAMES = ("w_in", "ssd_conv_w", "ssd_conv_b", "ssd_dt_bias", "ssd_a_log", "ssd_d", "ssd_norm_w",
                "fox_f_bias", "gate_bias", "w_proj_ssd", "w_proj_att", "w_out", "ln1_g", "ln1_b",
                "w_up", "ffn_conv_w", "ffn_conv_b", "w_down", "ln2_g", "ln2_b")


def kernel(x, w_in, ssd_conv_w, ssd_conv_b, ssd_dt_bias, ssd_a_log, ssd_d, ssd_norm_w, fox_f_bias, gate_bias,
           w_proj_ssd, w_proj_att, w_out, ln1_g, ln1_b, w_up, ffn_conv_w, ffn_conv_b, w_down, ln2_g, ln2_b):
    params = (w_in, ssd_conv_w, ssd_conv_b, ssd_dt_bias, ssd_a_log, ssd_d, ssd_norm_w, fox_f_bias, gate_bias,
              w_proj_ssd, w_proj_att, w_out, ln1_g, ln1_b, w_up, ffn_conv_w, ffn_conv_b, w_down, ln2_g, ln2_b)
    batch, length, d = x.shape
    depth = w_in.shape[0]
    alpha = (2.0 * depth) ** 0.25
    h = x.reshape(batch * length, d)
    for layer in range(depth):
        p = {name: arr[layer] for name, arr in zip(_PARAM_NAMES, params)}
        h = _layer(h, p, batch=batch, length=length, alpha=alpha)
    return h.reshape(batch, length, d)
```

```python
import functools
import math

import jax
import jax.numpy as jnp
from jax import lax
from jax.experimental import pallas as pl
from jax.experimental.pallas import tpu as pltpu

F32 = jnp.float32
BF16 = jnp.bfloat16

SSD_HEAD_DIM = 64
SSD_STATE = 128
FOX_HEAD_DIM = 128
LN_EPS = 1e-5
RMS_EPS = 1e-5

LANES = 128
SUBLANES = 8
BF16_SUBLANES = 16
VMEM_LIMIT_BYTES = 56 * 1024 * 1024

SSD_CHUNK = 128
ATT_BLOCK = 512
ATT_HEADS_PER_STEP = 4


def _cparams(*sem):
    return pltpu.CompilerParams(dimension_semantics=sem, vmem_limit_bytes=VMEM_LIMIT_BYTES)


def _tile(n, pref, quantum=LANES):
    if n <= pref:
        return n
    t = (pref // quantum) * quantum
    while t > quantum and n % t:
        t -= quantum
    assert n % t == 0, (n, pref, quantum)
    return t


def _softplus(x):
    return jnp.maximum(x, 0.0) + jnp.log1p(jnp.exp(-jnp.abs(x)))


def _log_sigmoid(x):
    return jnp.minimum(x, 0.0) - jnp.log1p(jnp.exp(-jnp.abs(x)))


def _silu(x):
    h = 0.5 * x
    return h + h * jnp.tanh(h)


def _proj_body(a_ref, wt_ref, *rest, scaled_tiles, scale, sigmoid):
    if sigmoid:
        b_ref, o_ref = rest
    else:
        (o_ref,) = rest
    acc = lax.dot_general(a_ref[...], wt_ref[...], (((1,), (1,)), ((), ())), preferred_element_type=F32)
    if scaled_tiles:
        acc = acc * jnp.where(pl.program_id(1) < scaled_tiles, scale, 1.0)
    if sigmoid:
        acc = jax.nn.sigmoid(acc + b_ref[...])
    o_ref[...] = acc.astype(o_ref.dtype)


def _proj(a, wt, out_dtype, *, row0=0, n=None, bias=None, scaled_cols=0, scale=1.0, tm=1024, tn=1024, name):
    m, k = a.shape
    n = wt.shape[0] if n is None else n
    tm, tn = _tile(m, tm), _tile(n, tn)
    assert scaled_cols % tn == 0 and row0 % BF16_SUBLANES == 0
    in_specs = [pl.BlockSpec((tm, k), lambda i, j: (i, 0)),
                pl.BlockSpec((pl.Element(tn), pl.Element(k)),
                             lambda i, j: (pl.multiple_of(row0 + j * tn, BF16_SUBLANES), 0))]
    args = [a, wt]
    if bias is not None:
        in_specs.append(pl.BlockSpec((1, tn), lambda i, j: (0, j)))
        args.append(bias)
    return pl.pallas_call(
        functools.partial(_proj_body, scaled_tiles=scaled_cols // tn, scale=scale, sigmoid=bias is not None),
        grid=(m // tm, n // tn),
        in_specs=in_specs,
        out_specs=pl.BlockSpec((tm, tn), lambda i, j: (i, j)),
        out_shape=jax.ShapeDtypeStruct((m, n), out_dtype),
        compiler_params=_cparams("parallel", "arbitrary"),
        name=name,
    )(*args)


def _head_prep_body(dt_ref, f_ref, dtb_ref, alog_ref, fb_ref, acs_ref, dtt_ref, acst_ref, fcum_ref, carry_ref, *,
                    fox_heads):
    @pl.when(pl.program_id(1) == 0)
    def _():
        carry_ref[...] = jnp.zeros_like(carry_ref)

    q = dt_ref.shape[0]
    row = lax.broadcasted_iota(jnp.int32, (q, q), 0)
    col = lax.broadcasted_iota(jnp.int32, (q, q), 1)
    tri = (row >= col).astype(F32)
    dtv = _softplus(dt_ref[...] + dtb_ref[...])
    da = dtv * (-jnp.exp(alog_ref[...]))
    acs = jnp.dot(tri, da, preferred_element_type=F32, precision=lax.Precision.HIGHEST)
    acs_ref[...] = acs
    dtt_ref[...] = dtv.T
    acst_ref[...] = acs.T
    logf = _log_sigmoid(f_ref[...] + fb_ref[...])
    cs = jnp.dot(tri, logf, preferred_element_type=F32, precision=lax.Precision.HIGHEST) + carry_ref[...]
    carry_ref[...] = cs[q - 1:q, :]
    fcum_ref[...] = cs.T[0:fox_heads, :]


def _head_prep(small, dt_bias, a_log, f_bias, *, batch, length, fox_heads):
    m = small.shape[0]
    q = SSD_CHUNK
    nc = length // q
    nat = lambda lane_blk: pl.BlockSpec((q, LANES), lambda b, c: (b * nc + c, lane_blk))
    tr = pl.BlockSpec((LANES, q), lambda b, c: (0, b * nc + c))
    vec = pl.BlockSpec((1, LANES), lambda b, c: (0, 0))
    return pl.pallas_call(
        functools.partial(_head_prep_body, fox_heads=fox_heads),
        grid=(batch, nc),
        in_specs=[nat(0), nat(1), vec, vec, vec],
        out_specs=[nat(0), tr, tr, pl.BlockSpec((None, fox_heads, q), lambda b, c: (b, 0, c))],
        out_shape=[jax.ShapeDtypeStruct((m, LANES), F32), jax.ShapeDtypeStruct((LANES, m), F32),
                   jax.ShapeDtypeStruct((LANES, m), F32), jax.ShapeDtypeStruct((batch, fox_heads, length), F32)],
        scratch_shapes=[pltpu.VMEM((1, LANES), F32)],
        compiler_params=_cparams("parallel", "arbitrary"),
        name="head_prep",
    )(small, small, dt_bias, a_log, f_bias)


def _ssd_body(z_ref, xs_ref, b_ref, c_ref, acs_ref, dtt_ref, acst_ref, px_ref, pb_ref, pc_ref, y_ref,
              h_ref, tail_ref, ybuf_ref, *, q, r, kw):
    g = pl.program_id(1)
    c = pl.program_id(2)
    gw = r * SSD_HEAD_DIM
    n = SSD_STATE

    @pl.when(c == 0)
    def _():
        h_ref[...] = jnp.zeros_like(h_ref)
        tail_ref[...] = jnp.zeros_like(tail_ref)

    def conv_silu(cur_ref, lo, hi, p_ref):
        cur = cur_ref[...].astype(F32)
        p = p_ref[...]
        bias = p[kw:kw + 1, :]
        out = bias + p[kw - 1:kw, :] * cur
        head_src = jnp.concatenate([tail_ref[:, lo:hi], cur[0:SUBLANES, :]], axis=0)
        head = bias + p[kw - 1:kw, :] * head_src[SUBLANES:2 * SUBLANES, :]
        for k in range(kw - 1):
            shift = kw - 1 - k
            out = out + p[k:k + 1, :] * pltpu.roll(cur, shift, axis=0)
            head = head + p[k:k + 1, :] * head_src[SUBLANES - shift:2 * SUBLANES - shift, :]
        tail_ref[:, lo:hi] = cur[q - SUBLANES:q, :]
        return _silu(jnp.concatenate([head, out[SUBLANES:, :]], axis=0))

    xs = conv_silu(xs_ref, 0, gw, px_ref)
    bm = conv_silu(b_ref, gw, gw + n, pb_ref)
    cm = conv_silu(c_ref, gw + n, gw + 2 * n, pc_ref)

    row = lax.broadcasted_iota(jnp.int32, (q, q), 0)
    col = lax.broadcasted_iota(jnp.int32, (q, q), 1)
    tri = row >= col
    acs_g = pltpu.roll(acs_ref[...], (LANES - g * r) % LANES, axis=1)
    dt_t = dtt_ref[...]
    acs_t = acst_ref[...]

    cm_bf = cm.astype(BF16)
    cb = lax.dot_general(cm_bf, bm.astype(BF16), (((1,), (1,)), ((), ())), preferred_element_type=F32)
    bm_t = bm.T

    lane = lax.broadcasted_iota(jnp.int32, (q, LANES), 1)
    lo_half = lane < SSD_HEAD_DIM
    lo_half_row = lo_half[0:1, :]
    pairs = range(r // 2)
    pair_cols = [slice(j * LANES, (j + 1) * LANES) for j in pairs]
    y_off = jnp.dot(cm_bf, h_ref[...].astype(BF16), preferred_element_type=F32)

    lhs_y, lhs_s, rhs, e_pair, cd_pair = [], [], [], [], []
    for j in pairs:
        xs_p = xs[:, pair_cols[j]]
        rhs.append(jnp.concatenate([jnp.where(lo_half, xs_p, 0.0).astype(BF16),
                                    jnp.where(lo_half, 0.0, xs_p).astype(BF16)], axis=0))
        m_parts, bw_parts, e_cols, cd = [], [], [], []
        for hd in (2 * j, 2 * j + 1):
            a_col = acs_g[:, hd:hd + 1]
            a_row = acs_t[hd:hd + 1, :]
            dt_row = dt_t[hd:hd + 1, :]
            a_last = acs_g[q - 1:q, hd:hd + 1]
            lmat = jnp.exp(jnp.where(tri, a_col - a_row, -jnp.inf))
            m_parts.append(cb * lmat * dt_row)
            bw_parts.append(bm_t * (jnp.exp(a_last - a_row) * dt_row))
            e_cols.append(jnp.exp(a_col))
            cd.append(jnp.exp(a_last))
        lhs_y.append(jnp.concatenate(m_parts, axis=1).astype(BF16))
        lhs_s.append(jnp.concatenate(bw_parts, axis=1).astype(BF16))
        e_pair.append(jnp.where(lo_half, e_cols[0], e_cols[1]))
        cd_pair.append(jnp.where(lo_half_row, cd[0], cd[1]))
    y_diag = [jnp.dot(lhs_y[j], rhs[j], preferred_element_type=F32) for j in pairs]
    s_new = [jnp.dot(lhs_s[j], rhs[j], preferred_element_type=F32) for j in pairs]
    ssq = jnp.zeros((q, 1), F32)
    for j in pairs:
        cols = pair_cols[j]
        h_ref[:, cols] = cd_pair[j] * h_ref[:, cols] + s_new[j]
        y = y_diag[j] + e_pair[j] * y_off[:, cols] + px_ref[kw + 1:kw + 2, cols] * xs[:, cols]
        y = y * _silu(z_ref[:, cols].astype(F32))
        ybuf_ref[:, cols] = y
        ssq = ssq + jnp.sum(y * y, axis=1, keepdims=True)
    inv = lax.rsqrt(ssq / gw + RMS_EPS)
    y_ref[...] = (ybuf_ref[...] * inv * px_ref[kw + 2:kw + 3, :]).astype(y_ref.dtype)


def _ssd(zx, acs, dt_t, acs_t, conv_w, conv_b, d_cols, norm_w, *, batch, length, d_inner, groups):
    m = zx.shape[0]
    q = SSD_CHUNK
    assert length % q == 0
    nc = length // q
    gw = d_inner // groups
    r = gw // SSD_HEAD_DIM
    assert r % 2 == 0 and r % SUBLANES == 0 and r * groups <= LANES
    kw = conv_w.shape[0]
    n = SSD_STATE
    conv_dim = conv_w.shape[1]
    rows = jnp.concatenate([conv_w, conv_b], axis=0)
    extra = jnp.zeros((2, conv_dim), F32).at[0, :d_inner].set(d_cols[0]).at[1, :d_inner].set(norm_w[0])
    params = jnp.concatenate([rows, extra], axis=0)
    np_rows = params.shape[0]
    zblk = d_inner // gw
    bblk = 2 * d_inner // n
    cblk_w = d_inner // n
    row = lambda b, g, c: b * nc + c
    in_specs = [
        pl.BlockSpec((q, gw), lambda b, g, c: (row(b, g, c), g)),
        pl.BlockSpec((q, gw), lambda b, g, c: (row(b, g, c), zblk + g)),
        pl.BlockSpec((q, n), lambda b, g, c: (row(b, g, c), bblk + g)),
        pl.BlockSpec((q, n), lambda b, g, c: (row(b, g, c), bblk + groups + g)),
        pl.BlockSpec((q, LANES), lambda b, g, c: (row(b, g, c), 0)),
        pl.BlockSpec((r, q), lambda b, g, c: (g, row(b, g, c))),
        pl.BlockSpec((r, q), lambda b, g, c: (g, row(b, g, c))),
        pl.BlockSpec((np_rows, gw), lambda b, g, c: (0, g)),
        pl.BlockSpec((np_rows, n), lambda b, g, c: (0, cblk_w + g)),
        pl.BlockSpec((np_rows, n), lambda b, g, c: (0, cblk_w + groups + g)),
    ]
    return pl.pallas_call(
        functools.partial(_ssd_body, q=q, r=r, kw=kw),
        grid=(batch, groups, nc),
        in_specs=in_specs,
        out_specs=pl.BlockSpec((q, gw), lambda b, g, c: (row(b, g, c), g)),
        out_shape=jax.ShapeDtypeStruct((m, d_inner), BF16),
        scratch_shapes=[
            pltpu.VMEM((n, gw), F32),
            pltpu.VMEM((SUBLANES, gw + 2 * n), F32),
            pltpu.VMEM((q, gw), F32),
        ],
        compiler_params=_cparams("parallel", "parallel", "arbitrary"),
        name="ssd_scan",
    )(zx, zx, zx, zx, acs, dt_t, acs_t, params, params, params)


def _attn_body(q_ref, k_ref, v_ref, f_ref, o_ref, vaug_ref, m_ref, acc_ref, *, blk, hp):
    qi = pl.program_id(2)
    dh = FOX_HEAD_DIM
    length = k_ref.shape[0]

    @pl.when(qi == 0)
    def _():
        for h in range(hp):
            vaug_ref[h, :, 0:dh] = v_ref[:, h * dh:(h + 1) * dh]
            vaug_ref[h, :, dh:2 * dh] = jnp.ones((length, dh), BF16)

    m_ref[...] = jnp.full_like(m_ref, -jnp.inf)
    acc_ref[...] = jnp.zeros_like(acc_ref)

    def step(j, masked):
        start = pl.multiple_of(j * blk, blk)

        def scores(h):
            qv = q_ref[:, h * dh:(h + 1) * dh]
            kj = k_ref[pl.ds(start, blk), h * dh:(h + 1) * dh]
            return lax.dot_general(qv, kj, (((1,), (1,)), ((), ())), preferred_element_type=F32)

        if masked:
            causal = (lax.broadcasted_iota(jnp.int32, (blk, blk), 0)
                      >= lax.broadcasted_iota(jnp.int32, (blk, blk), 1))

        def softmax_part(h, s):
            s = s - f_ref[h, pl.ds(j, 1), :]
            if masked:
                s = jnp.where(causal, s, -jnp.inf)
            m_old = m_ref[h]
            m_new = jnp.maximum(m_old, jnp.max(s, axis=1, keepdims=True))
            m_ref[h] = m_new
            return jnp.exp(s - jnp.tile(m_new, (1, blk // LANES))).astype(BF16), jnp.exp(m_old - m_new)

        def accumulate(h, p, alpha):
            pv = jnp.dot(p, vaug_ref[h, pl.ds(start, blk), :], preferred_element_type=F32)
            acc_ref[h] = jnp.tile(alpha, (1, 2)) * acc_ref[h] + pv

        s_cur = scores(0)
        pending = None
        for h in range(hp):
            s_nxt = scores(h + 1) if h + 1 < hp else None
            p, alpha = softmax_part(h, s_cur)
            if pending is not None:
                accumulate(*pending)
            pending = (h, p, alpha)
            s_cur = s_nxt
        accumulate(*pending)

    def loop_body(j, carry):
        step(j, False)
        return carry

    lax.fori_loop(0, qi, loop_body, 0)
    step(qi, True)
    for h in range(hp):
        acc = acc_ref[h]
        o_ref[:, h * dh:(h + 1) * dh] = (acc[:, 0:dh] / acc[:, dh:2 * dh]).astype(o_ref.dtype)


def _attention(qkv, fcum_t, *, batch, length, heads):
    m = qkv.shape[0]
    dh = FOX_HEAD_DIM
    hp = ATT_HEADS_PER_STEP
    assert dh == LANES and heads % hp == 0
    blk = _tile(length, ATT_BLOCK)
    nq = length // blk
    ng = heads // hp
    fcum_t = fcum_t.reshape(batch * ng, hp, nq, blk)
    return pl.pallas_call(
        functools.partial(_attn_body, blk=blk, hp=hp),
        grid=(batch, ng, nq),
        in_specs=[
            pl.BlockSpec((blk, hp * dh), lambda b, g, i: (b * nq + i, g)),
            pl.BlockSpec((length, hp * dh), lambda b, g, i: (b, ng + g)),
            pl.BlockSpec((length, hp * dh), lambda b, g, i: (b, 2 * ng + g)),
            pl.BlockSpec((None, hp, nq, blk), lambda b, g, i: (b * ng + g, 0, 0, 0)),
        ],
        out_specs=pl.BlockSpec((blk, hp * dh), lambda b, g, i: (b * nq + i, g)),
        out_shape=jax.ShapeDtypeStruct((m, heads * dh), BF16),
        scratch_shapes=[pltpu.VMEM((hp, length, 2 * dh), BF16),
                        pltpu.VMEM((hp, blk, LANES), F32),
                        pltpu.VMEM((hp, blk, 2 * dh), F32)],
        compiler_params=_cparams("parallel", "parallel", "arbitrary"),
        name="fox_attention",
    )(qkv, qkv, qkv, fcum_t)


def _merge_body(ys_ref, ya_ref, ws_ref, wa_ref, gs_ref, ga_ref, o_ref):
    ps = jnp.dot(ys_ref[...], ws_ref[...], preferred_element_type=F32)
    pa = jnp.dot(ya_ref[...], wa_ref[...], preferred_element_type=F32)
    o_ref[...] = (gs_ref[...].astype(F32) * ps + ga_ref[...].astype(F32) * pa).astype(o_ref.dtype)


def _merge(y_ssd, y_att, w_ssd, w_att, gates, *, tm=1024, tn=256):
    m, ks = y_ssd.shape
    ka = y_att.shape[1]
    n = w_ssd.shape[1]
    tm, tn = _tile(m, tm), _tile(n, tn)
    nj = n // tn
    once = pl.Buffered(1)
    return pl.pallas_call(
        _merge_body,
        grid=(m // tm, nj),
        in_specs=[
            pl.BlockSpec((tm, ks), lambda i, j: (i, 0), pipeline_mode=once),
            pl.BlockSpec((tm, ka), lambda i, j: (i, 0), pipeline_mode=once),
            pl.BlockSpec((ks, tn), lambda i, j: (0, j)),
            pl.BlockSpec((ka, tn), lambda i, j: (0, j)),
            pl.BlockSpec((tm, tn), lambda i, j: (i, j)),
            pl.BlockSpec((tm, tn), lambda i, j: (i, nj + j)),
        ],
        out_specs=pl.BlockSpec((tm, tn), lambda i, j: (i, j)),
        out_shape=jax.ShapeDtypeStruct((m, n), BF16),
        compiler_params=_cparams("parallel", "arbitrary"),
        name="merge_proj",
    )(y_ssd, y_att, w_ssd, w_att, gates, gates)


def _mm_residual_body(a_ref, w_ref, r_ref, o_ref, *, alpha):
    acc = jnp.dot(a_ref[...], w_ref[...], preferred_element_type=F32)
    o_ref[...] = alpha * r_ref[...] + acc


def _matmul_residual(a, w, resid, alpha, *, tm, tn, lhs_buffers=2, name):
    m, k = a.shape
    n = w.shape[1]
    tm, tn = _tile(m, tm), _tile(n, tn)
    return pl.pallas_call(
        functools.partial(_mm_residual_body, alpha=alpha),
        grid=(m // tm, n // tn),
        in_specs=[
            pl.BlockSpec((tm, k), lambda i, j: (i, 0), pipeline_mode=pl.Buffered(lhs_buffers)),
            pl.BlockSpec((k, tn), lambda i, j: (0, j)),
            pl.BlockSpec((tm, tn), lambda i, j: (i, j)),
        ],
        out_specs=pl.BlockSpec((tm, tn), lambda i, j: (i, j)),
        out_shape=jax.ShapeDtypeStruct((m, n), F32),
        compiler_params=_cparams("parallel", "arbitrary"),
        name=name,
    )(a, w, resid)


def _ln_body(x_ref, g_ref, b_ref, *o_refs):
    x = x_ref[...]
    mu = jnp.mean(x, axis=-1, keepdims=True)
    xc = x - mu
    var = jnp.mean(xc * xc, axis=-1, keepdims=True)
    out = xc * lax.rsqrt(var + LN_EPS) * g_ref[...] + b_ref[...]
    for o_ref in o_refs:
        o_ref[...] = out.astype(o_ref.dtype)


def _layer_norm(x, gain, bias, out_dtypes, *, tm=256, name):
    m, d = x.shape
    tm = _tile(m, tm, SUBLANES)
    outs = pl.pallas_call(
        _ln_body,
        grid=(m // tm,),
        in_specs=[pl.BlockSpec((tm, d), lambda i: (i, 0)),
                  pl.BlockSpec((1, d), lambda i: (0, 0)),
                  pl.BlockSpec((1, d), lambda i: (0, 0))],
        out_specs=[pl.BlockSpec((tm, d), lambda i: (i, 0)) for _ in out_dtypes],
        out_shape=[jax.ShapeDtypeStruct((m, d), dt) for dt in out_dtypes],
        compiler_params=_cparams("parallel"),
        name=name,
    )(x, gain, bias)
    return outs


def _ffn_up_body(a_ref, halo_ref, wv_ref, wg_ref, cwv_ref, cwg_ref, cbv_ref, cbg_ref, o_ref,
                 wbf_ref, *, kw, tiles_per_seq):
    i = pl.program_id(1)
    pad = BF16_SUBLANES

    @pl.when(i == 0)
    def _():
        wbf_ref[0] = wv_ref[...].astype(BF16)
        wbf_ref[1] = wg_ref[...].astype(BF16)

    a = a_ref[...]
    halo = halo_ref[...]
    keep = (i % tiles_per_seq != 0).astype(F32)

    def conv(c, cw_ref, cb_ref):
        w = wbf_ref[c]
        u = jnp.dot(a, w, preferred_element_type=F32)
        uh = jnp.dot(halo, w, preferred_element_type=F32) * keep
        cw = cw_ref[...]
        cb = cb_ref[...]
        out = cb + cw[kw - 1:kw, :] * u
        head_src = jnp.concatenate([uh, u[0:pad, :]], axis=0)
        head = cb + cw[kw - 1:kw, :] * head_src[pad:2 * pad, :]
        for k in range(kw - 1):
            shift = kw - 1 - k
            out = out + cw[k:k + 1, :] * pltpu.roll(u, shift, axis=0)
            head = head + cw[k:k + 1, :] * head_src[pad - shift:2 * pad - shift, :]
        return out, head

    gate, gate_head = conv(1, cwg_ref, cbg_ref)
    gate, gate_head = _silu(gate), _silu(gate_head)
    val, val_head = conv(0, cwv_ref, cbv_ref)
    o_ref[...] = (gate * val).astype(o_ref.dtype)
    o_ref[0:pad, :] = (gate_head * val_head).astype(o_ref.dtype)


def _ffn_up(h, w_up, conv_w, conv_b, *, length, d_ff, tm=1024, tn=256):
    m, k = h.shape
    tm = _tile(min(m, length), tm, BF16_SUBLANES)
    assert length % tm == 0
    tn = _tile(d_ff, tn)
    nj = d_ff // tn
    kw = conv_w.shape[0]
    hb = tm // BF16_SUBLANES
    return pl.pallas_call(
        functools.partial(_ffn_up_body, kw=kw, tiles_per_seq=length // tm),
        grid=(nj, m // tm),
        in_specs=[
            pl.BlockSpec((tm, k), lambda j, i: (i, 0)),
            pl.BlockSpec((BF16_SUBLANES, k), lambda j, i: (jnp.maximum(i * hb - 1, 0), 0)),
            pl.BlockSpec((k, tn), lambda j, i: (0, j)),
            pl.BlockSpec((k, tn), lambda j, i: (0, nj + j)),
            pl.BlockSpec((kw, tn), lambda j, i: (0, j)),
            pl.BlockSpec((kw, tn), lambda j, i: (0, nj + j)),
            pl.BlockSpec((1, tn), lambda j, i: (0, j)),
            pl.BlockSpec((1, tn), lambda j, i: (0, nj + j)),
        ],
        out_specs=pl.BlockSpec((tm, tn), lambda j, i: (i, j)),
        out_shape=jax.ShapeDtypeStruct((m, d_ff), BF16),
        scratch_shapes=[pltpu.VMEM((2, k, tn), BF16)],
        compiler_params=_cparams("parallel", "arbitrary"),
        name="ffn_up_conv_act",
    )(h, h, w_up, w_up, conv_w, conv_w, conv_b, conv_b)


def _layer(h, p, *, batch, length, alpha):
    m, d = h.shape
    d_inner = p["ssd_norm_w"].shape[-1]
    conv_dim = p["ssd_conv_b"].shape[-1]
    ssd_heads = p["ssd_dt_bias"].shape[-1]
    fox_heads = p["fox_f_bias"].shape[-1]
    d_att = fox_heads * FOX_HEAD_DIM
    d_ff = p["w_down"].shape[0]
    groups = (conv_dim - d_inner) // (2 * SSD_STATE)
    assert ssd_heads <= LANES and fox_heads <= LANES

    o_z, o_xbc = 0, d_inner
    o_dt = o_xbc + conv_dim
    o_q = o_dt + ssd_heads
    o_f = o_q + 3 * d_att
    o_g = o_f + fox_heads
    wt_in = p["w_in"].T.astype(BF16)
    zeros = lambda n: jnp.zeros((n, d), BF16)
    wt_small = jnp.concatenate([wt_in[o_dt:o_q], zeros(LANES - ssd_heads),
                                wt_in[o_f:o_g], zeros(LANES - fox_heads)], axis=0)

    h_bf = h.astype(BF16)
    zx = _proj(h_bf, wt_in, F32, row0=o_z, n=o_dt, name="in_proj_zx")
    qkv = _proj(h_bf, wt_in, BF16, row0=o_q, n=3 * d_att, scaled_cols=d_att, scale=1.0 / math.sqrt(FOX_HEAD_DIM),
                name="in_proj_qkv")
    gates = _proj(h_bf, wt_in, F32, row0=o_g, n=2 * d, bias=p["gate_bias"].reshape(1, 2 * d), name="in_proj_gates")
    small = _proj(h_bf, wt_small, F32, name="in_proj_small")

    pad_row = lambda v, n: jnp.pad(v.reshape(1, -1).astype(F32), ((0, 0), (0, n - v.shape[-1])))
    acs, dt_t, acs_t, fcum = _head_prep(small, pad_row(p["ssd_dt_bias"], LANES), pad_row(p["ssd_a_log"], LANES),
                                        pad_row(p["fox_f_bias"], LANES),
                                        batch=batch, length=length, fox_heads=fox_heads)
    y_ssd = _ssd(zx, acs, dt_t, acs_t, p["ssd_conv_w"], p["ssd_conv_b"].reshape(1, -1),
                 jnp.repeat(p["ssd_d"].astype(F32), SSD_HEAD_DIM).reshape(1, -1),
                 p["ssd_norm_w"].reshape(1, -1),
                 batch=batch, length=length, d_inner=d_inner, groups=groups)
    y_att = _attention(qkv, fcum, batch=batch, length=length, heads=fox_heads)

    merged = _merge(y_ssd, y_att, p["w_proj_ssd"].astype(BF16), p["w_proj_att"].astype(BF16), gates)
    pre1 = _matmul_residual(merged, p["w_out"].astype(BF16), h, alpha, tm=1024, tn=512, name="out_proj_residual")
    h1, h1_bf = _layer_norm(pre1, p["ln1_g"].reshape(1, -1), p["ln1_b"].reshape(1, -1), (F32, BF16), name="layer_norm_1")

    act = _ffn_up(h1_bf, p["w_up"], p["ffn_conv_w"], p["ffn_conv_b"].reshape(1, -1),
                  length=length, d_ff=d_ff)
    pre2 = _matmul_residual(act, p["w_down"].astype(BF16), h1, alpha, tm=1024, tn=256, lhs_buffers=1,
                            name="ffn_down_residual")
    (out,) = _layer_norm(pre2, p["ln2_g"].reshape(1, -1), p["ln2_b"].reshape(1, -1), (F32,), name="layer_norm_2")
    return out


_PARAM_NAMES = ("w_in", "ssd_conv_w", "ssd_conv_b", "ssd_dt_bias", "ssd_a_log", "ssd_d", "ssd_norm_w",
                "fox_f_bias", "gate_bias", "w_proj_ssd", "w_proj_att", "w_out", "ln1_g", "ln1_b",
                "w_up", "ffn_conv_w", "ffn_conv_b", "w_down", "ln2_g", "ln2_b")


def kernel(x, w_in, ssd_conv_w, ssd_conv_b, ssd_dt_bias, ssd_a_log, ssd_d, ssd_norm_w, fox_f_bias, gate_bias,
           w_proj_ssd, w_proj_att, w_out, ln1_g, ln1_b, w_up, ffn_conv_w, ffn_conv_b, w_down, ln2_g, ln2_b):
    params = (w_in, ssd_conv_w, ssd_conv_b, ssd_dt_bias, ssd_a_log, ssd_d, ssd_norm_w, fox_f_bias, gate_bias,
              w_proj_ssd, w_proj_att, w_out, ln1_g, ln1_b, w_up, ffn_conv_w, ffn_conv_b, w_down, ln2_g, ln2_b)
    batch, length, d = x.shape
    depth = w_in.shape[0]
    alpha = (2.0 * depth) ** 0.25
    h = x.reshape(batch * length, d)
    for layer in range(depth):
        p = {name: arr[layer] for name, arr in zip(_PARAM_NAMES, params)}
        h = _layer(h, p, batch=batch, length=length, alpha=alpha)
    return h.reshape(batch, length, d)
```

```python
import functools
import math

import jax
import jax.numpy as jnp
from jax import lax
from jax.experimental import pallas as pl
from jax.experimental.pallas import tpu as pltpu

F32 = jnp.float32
BF16 = jnp.bfloat16

SSD_HEAD_DIM = 64
SSD_STATE = 128
FOX_HEAD_DIM = 128
LN_EPS = 1e-5
RMS_EPS = 1e-5

LANES = 128
SUBLANES = 8
BF16_SUBLANES = 16
VMEM_LIMIT_BYTES = 56 * 1024 * 1024

SSD_CHUNK = 128
ATT_BLOCK = 512
ATT_HEADS_PER_STEP = 4


def _cparams(*sem):
    return pltpu.CompilerParams(dimension_semantics=sem, vmem_limit_bytes=VMEM_LIMIT_BYTES)


def _tile(n, pref, quantum=LANES):
    if n <= pref:
        return n
    t = (pref // quantum) * quantum
    while t > quantum and n % t:
        t -= quantum
    assert n % t == 0, (n, pref, quantum)
    return t


def _cast_slices(weights, steps, linear_step):
    specs, shapes = [], []
    for w in weights:
        rows, cols = w.shape
        assert rows % steps == 0 and (rows // steps) % BF16_SUBLANES == 0, (w.shape, steps)
        specs.append(pl.BlockSpec((rows // steps, cols), lambda *ids: (linear_step(*ids), 0)))
        shapes.append(jax.ShapeDtypeStruct((rows, cols), BF16))
    return specs, shapes


def _softplus(x):
    return jnp.maximum(x, 0.0) + jnp.log1p(jnp.exp(-jnp.abs(x)))


def _log_sigmoid(x):
    return jnp.minimum(x, 0.0) - jnp.log1p(jnp.exp(-jnp.abs(x)))


def _silu(x):
    h = 0.5 * x
    return h + h * jnp.tanh(h)


def _proj_body(a_ref, wt_ref, *rest, scaled_tiles, scale, sigmoid):
    if sigmoid:
        b_ref, o_ref = rest
    else:
        (o_ref,) = rest
    acc = lax.dot_general(a_ref[...], wt_ref[...], (((1,), (1,)), ((), ())), preferred_element_type=F32)
    if scaled_tiles:
        acc = acc * jnp.where(pl.program_id(1) < scaled_tiles, scale, 1.0)
    if sigmoid:
        acc = jax.nn.sigmoid(acc + b_ref[...])
    o_ref[...] = acc.astype(o_ref.dtype)


def _proj(a, wt, out_dtype, *, row0=0, n=None, bias=None, scaled_cols=0, scale=1.0, tm=1024, tn=1024, name):
    m, k = a.shape
    n = wt.shape[0] if n is None else n
    tm, tn = _tile(m, tm), _tile(n, tn)
    assert scaled_cols % tn == 0 and row0 % BF16_SUBLANES == 0
    in_specs = [pl.BlockSpec((tm, k), lambda i, j: (i, 0)),
                pl.BlockSpec((pl.Element(tn), pl.Element(k)),
                             lambda i, j: (pl.multiple_of(row0 + j * tn, BF16_SUBLANES), 0))]
    args = [a, wt]
    if bias is not None:
        in_specs.append(pl.BlockSpec((1, tn), lambda i, j: (0, j)))
        args.append(bias)
    return pl.pallas_call(
        functools.partial(_proj_body, scaled_tiles=scaled_cols // tn, scale=scale, sigmoid=bias is not None),
        grid=(m // tm, n // tn),
        in_specs=in_specs,
        out_specs=pl.BlockSpec((tm, tn), lambda i, j: (i, j)),
        out_shape=jax.ShapeDtypeStruct((m, n), out_dtype),
        compiler_params=_cparams("parallel", "arbitrary"),
        name=name,
    )(*args)


def _head_prep_body(dt_ref, f_ref, dtb_ref, alog_ref, fb_ref, acs_ref, dtt_ref, acst_ref, fcum_ref, carry_ref, *,
                    fox_heads):
    @pl.when(pl.program_id(1) == 0)
    def _():
        carry_ref[...] = jnp.zeros_like(carry_ref)

    q = dt_ref.shape[0]
    row = lax.broadcasted_iota(jnp.int32, (q, q), 0)
    col = lax.broadcasted_iota(jnp.int32, (q, q), 1)
    tri = (row >= col).astype(F32)
    dtv = _softplus(dt_ref[...] + dtb_ref[...])
    da = dtv * (-jnp.exp(alog_ref[...]))
    acs = jnp.dot(tri, da, preferred_element_type=F32, precision=lax.Precision.HIGHEST)
    acs_ref[...] = acs
    dtt_ref[...] = dtv.T
    acst_ref[...] = acs.T
    logf = _log_sigmoid(f_ref[...] + fb_ref[...])
    cs = jnp.dot(tri, logf, preferred_element_type=F32, precision=lax.Precision.HIGHEST) + carry_ref[...]
    carry_ref[...] = cs[q - 1:q, :]
    fcum_ref[...] = cs.T[0:fox_heads, :]


def _head_prep(small, dt_bias, a_log, f_bias, *, batch, length, fox_heads):
    m = small.shape[0]
    q = SSD_CHUNK
    nc = length // q
    nat = lambda lane_blk: pl.BlockSpec((q, LANES), lambda b, c: (b * nc + c, lane_blk))
    tr = pl.BlockSpec((LANES, q), lambda b, c: (0, b * nc + c))
    vec = pl.BlockSpec((1, LANES), lambda b, c: (0, 0))
    return pl.pallas_call(
        functools.partial(_head_prep_body, fox_heads=fox_heads),
        grid=(batch, nc),
        in_specs=[nat(0), nat(1), vec, vec, vec],
        out_specs=[nat(0), tr, tr, pl.BlockSpec((None, fox_heads, q), lambda b, c: (b, 0, c))],
        out_shape=[jax.ShapeDtypeStruct((m, LANES), F32), jax.ShapeDtypeStruct((LANES, m), F32),
                   jax.ShapeDtypeStruct((LANES, m), F32), jax.ShapeDtypeStruct((batch, fox_heads, length), F32)],
        scratch_shapes=[pltpu.VMEM((1, LANES), F32)],
        compiler_params=_cparams("parallel", "arbitrary"),
        name="head_prep",
    )(small, small, dt_bias, a_log, f_bias)


def _ssd_body(z_ref, xs_ref, b_ref, c_ref, acs_ref, dtt_ref, acst_ref, px_ref, pb_ref, pc_ref, y_ref,
              h_ref, tail_ref, ybuf_ref, *, q, r, kw):
    g = pl.program_id(1)
    c = pl.program_id(2)
    gw = r * SSD_HEAD_DIM
    n = SSD_STATE

    @pl.when(c == 0)
    def _():
        h_ref[...] = jnp.zeros_like(h_ref)
        tail_ref[...] = jnp.zeros_like(tail_ref)

    def conv_silu(cur_ref, lo, hi, p_ref):
        cur = cur_ref[...].astype(F32)
        p = p_ref[...]
        bias = p[kw:kw + 1, :]
        out = bias + p[kw - 1:kw, :] * cur
        head_src = jnp.concatenate([tail_ref[:, lo:hi], cur[0:SUBLANES, :]], axis=0)
        head = bias + p[kw - 1:kw, :] * head_src[SUBLANES:2 * SUBLANES, :]
        for k in range(kw - 1):
            shift = kw - 1 - k
            out = out + p[k:k + 1, :] * pltpu.roll(cur, shift, axis=0)
            head = head + p[k:k + 1, :] * head_src[SUBLANES - shift:2 * SUBLANES - shift, :]
        tail_ref[:, lo:hi] = cur[q - SUBLANES:q, :]
        return _silu(jnp.concatenate([head, out[SUBLANES:, :]], axis=0))

    xs = conv_silu(xs_ref, 0, gw, px_ref)
    bm = conv_silu(b_ref, gw, gw + n, pb_ref)
    cm = conv_silu(c_ref, gw + n, gw + 2 * n, pc_ref)

    row = lax.broadcasted_iota(jnp.int32, (q, q), 0)
    col = lax.broadcasted_iota(jnp.int32, (q, q), 1)
    tri = row >= col
    acs_g = pltpu.roll(acs_ref[...], (LANES - g * r) % LANES, axis=1)
    dt_t = dtt_ref[...]
    acs_t = acst_ref[...]

    cm_bf = cm.astype(BF16)
    cb = lax.dot_general(cm_bf, bm.astype(BF16), (((1,), (1,)), ((), ())), preferred_element_type=F32)
    bm_t = bm.T

    lane = lax.broadcasted_iota(jnp.int32, (q, LANES), 1)
    lo_half = lane < SSD_HEAD_DIM
    lo_half_row = lo_half[0:1, :]
    pairs = range(r // 2)
    pair_cols = [slice(j * LANES, (j + 1) * LANES) for j in pairs]
    y_off = jnp.dot(cm_bf, h_ref[...].astype(BF16), preferred_element_type=F32)

    lhs_y, lhs_s, rhs, e_pair, cd_pair = [], [], [], [], []
    for j in pairs:
        xs_p = xs[:, pair_cols[j]]
        rhs.append(jnp.concatenate([jnp.where(lo_half, xs_p, 0.0).astype(BF16),
                                    jnp.where(lo_half, 0.0, xs_p).astype(BF16)], axis=0))
        m_parts, bw_parts, e_cols, cd = [], [], [], []
        for hd in (2 * j, 2 * j + 1):
            a_col = acs_g[:, hd:hd + 1]
            a_row = acs_t[hd:hd + 1, :]
            dt_row = dt_t[hd:hd + 1, :]
            a_last = acs_g[q - 1:q, hd:hd + 1]
            lmat = jnp.exp(jnp.where(tri, a_col - a_row, -jnp.inf))
            m_parts.append(cb * lmat * dt_row)
            bw_parts.append(bm_t * (jnp.exp(a_last - a_row) * dt_row))
            e_cols.append(jnp.exp(a_col))
            cd.append(jnp.exp(a_last))
        lhs_y.append(jnp.concatenate(m_parts, axis=1).astype(BF16))
        lhs_s.append(jnp.concatenate(bw_parts, axis=1).astype(BF16))
        e_pair.append(jnp.where(lo_half, e_cols[0], e_cols[1]))
        cd_pair.append(jnp.where(lo_half_row, cd[0], cd[1]))
    y_diag = [jnp.dot(lhs_y[j], rhs[j], preferred_element_type=F32) for j in pairs]
    s_new = [jnp.dot(lhs_s[j], rhs[j], preferred_element_type=F32) for j in pairs]
    ssq = jnp.zeros((q, 1), F32)
    for j in pairs:
        cols = pair_cols[j]
        h_ref[:, cols] = cd_pair[j] * h_ref[:, cols] + s_new[j]
        y = y_diag[j] + e_pair[j] * y_off[:, cols] + px_ref[kw + 1:kw + 2, cols] * xs[:, cols]
        y = y * _silu(z_ref[:, cols].astype(F32))
        ybuf_ref[:, cols] = y
        ssq = ssq + jnp.sum(y * y, axis=1, keepdims=True)
    inv = lax.rsqrt(ssq / gw + RMS_EPS)
    y_ref[...] = (ybuf_ref[...] * inv * px_ref[kw + 2:kw + 3, :]).astype(y_ref.dtype)


def _ssd(zx, acs, dt_t, acs_t, conv_w, conv_b, d_cols, norm_w, *, batch, length, d_inner, groups):
    m = zx.shape[0]
    q = SSD_CHUNK
    assert length % q == 0
    nc = length // q
    gw = d_inner // groups
    r = gw // SSD_HEAD_DIM
    assert r % 2 == 0 and r % SUBLANES == 0 and r * groups <= LANES
    kw = conv_w.shape[0]
    n = SSD_STATE
    conv_dim = conv_w.shape[1]
    rows = jnp.concatenate([conv_w, conv_b], axis=0)
    extra = jnp.zeros((2, conv_dim), F32).at[0, :d_inner].set(d_cols[0]).at[1, :d_inner].set(norm_w[0])
    params = jnp.concatenate([rows, extra], axis=0)
    np_rows = params.shape[0]
    zblk = d_inner // gw
    bblk = 2 * d_inner // n
    cblk_w = d_inner // n
    row = lambda b, g, c: b * nc + c
    in_specs = [
        pl.BlockSpec((q, gw), lambda b, g, c: (row(b, g, c), g)),
        pl.BlockSpec((q, gw), lambda b, g, c: (row(b, g, c), zblk + g)),
        pl.BlockSpec((q, n), lambda b, g, c: (row(b, g, c), bblk + g)),
        pl.BlockSpec((q, n), lambda b, g, c: (row(b, g, c), bblk + groups + g)),
        pl.BlockSpec((q, LANES), lambda b, g, c: (row(b, g, c), 0)),
        pl.BlockSpec((r, q), lambda b, g, c: (g, row(b, g, c))),
        pl.BlockSpec((r, q), lambda b, g, c: (g, row(b, g, c))),
        pl.BlockSpec((np_rows, gw), lambda b, g, c: (0, g)),
        pl.BlockSpec((np_rows, n), lambda b, g, c: (0, cblk_w + g)),
        pl.BlockSpec((np_rows, n), lambda b, g, c: (0, cblk_w + groups + g)),
    ]
    return pl.pallas_call(
        functools.partial(_ssd_body, q=q, r=r, kw=kw),
        grid=(batch, groups, nc),
        in_specs=in_specs,
        out_specs=pl.BlockSpec((q, gw), lambda b, g, c: (row(b, g, c), g)),
        out_shape=jax.ShapeDtypeStruct((m, d_inner), BF16),
        scratch_shapes=[
            pltpu.VMEM((n, gw), F32),
            pltpu.VMEM((SUBLANES, gw + 2 * n), F32),
            pltpu.VMEM((q, gw), F32),
        ],
        compiler_params=_cparams("parallel", "parallel", "arbitrary"),
        name="ssd_scan",
    )(zx, zx, zx, zx, acs, dt_t, acs_t, params, params, params)


def _attn_body(q_ref, k_ref, v_ref, f_ref, *rest, blk, hp, n_cast):
    cast_in, o_ref, cast_out = rest[:n_cast], rest[n_cast], rest[n_cast + 1:2 * n_cast + 1]
    vaug_ref, m_ref, acc_ref = rest[2 * n_cast + 1:]
    for src, dst in zip(cast_in, cast_out):
        dst[...] = src[...].astype(BF16)
    qi = pl.program_id(2)
    dh = FOX_HEAD_DIM
    length = k_ref.shape[0]

    @pl.when(qi == 0)
    def _():
        for h in range(hp):
            vaug_ref[h, :, 0:dh] = v_ref[:, h * dh:(h + 1) * dh]
            vaug_ref[h, :, dh:2 * dh] = jnp.ones((length, dh), BF16)

    m_ref[...] = jnp.full_like(m_ref, -jnp.inf)
    acc_ref[...] = jnp.zeros_like(acc_ref)

    def step(j, masked):
        start = pl.multiple_of(j * blk, blk)

        def scores(h):
            qv = q_ref[:, h * dh:(h + 1) * dh]
            kj = k_ref[pl.ds(start, blk), h * dh:(h + 1) * dh]
            return lax.dot_general(qv, kj, (((1,), (1,)), ((), ())), preferred_element_type=F32)

        if masked:
            causal = (lax.broadcasted_iota(jnp.int32, (blk, blk), 0)
                      >= lax.broadcasted_iota(jnp.int32, (blk, blk), 1))

        def softmax_part(h, s):
            s = s - f_ref[h, pl.ds(j, 1), :]
            if masked:
                s = jnp.where(causal, s, -jnp.inf)
            m_old = m_ref[h]
            m_new = jnp.maximum(m_old, jnp.max(s, axis=1, keepdims=True))
            m_ref[h] = m_new
            return jnp.exp(s - jnp.tile(m_new, (1, blk // LANES))).astype(BF16), jnp.exp(m_old - m_new)

        def accumulate(h, p, alpha):
            pv = jnp.dot(p, vaug_ref[h, pl.ds(start, blk), :], preferred_element_type=F32)
            acc_ref[h] = jnp.tile(alpha, (1, 2)) * acc_ref[h] + pv

        s_cur = scores(0)
        pending = None
        for h in range(hp):
            s_nxt = scores(h + 1) if h + 1 < hp else None
            p, alpha = softmax_part(h, s_cur)
            if pending is not None:
                accumulate(*pending)
            pending = (h, p, alpha)
            s_cur = s_nxt
        accumulate(*pending)

    def loop_body(j, carry):
        step(j, False)
        return carry

    lax.fori_loop(0, qi, loop_body, 0)
    step(qi, True)
    for h in range(hp):
        acc = acc_ref[h]
        o_ref[:, h * dh:(h + 1) * dh] = (acc[:, 0:dh] / acc[:, dh:2 * dh]).astype(o_ref.dtype)


def _attention(qkv, fcum_t, cast_weights, *, batch, length, heads):
    m = qkv.shape[0]
    dh = FOX_HEAD_DIM
    hp = ATT_HEADS_PER_STEP
    assert dh == LANES and heads % hp == 0
    blk = _tile(length, ATT_BLOCK)
    nq = length // blk
    ng = heads // hp
    fcum_t = fcum_t.reshape(batch * ng, hp, nq, blk)
    cast_specs, cast_shapes = _cast_slices(cast_weights, batch * ng * nq, lambda b, g, i: (b * ng + g) * nq + i)
    outs = pl.pallas_call(
        functools.partial(_attn_body, blk=blk, hp=hp, n_cast=len(cast_weights)),
        grid=(batch, ng, nq),
        in_specs=[
            pl.BlockSpec((blk, hp * dh), lambda b, g, i: (b * nq + i, g)),
            pl.BlockSpec((length, hp * dh), lambda b, g, i: (b, ng + g)),
            pl.BlockSpec((length, hp * dh), lambda b, g, i: (b, 2 * ng + g)),
            pl.BlockSpec((None, hp, nq, blk), lambda b, g, i: (b * ng + g, 0, 0, 0)),
        ] + cast_specs,
        out_specs=[pl.BlockSpec((blk, hp * dh), lambda b, g, i: (b * nq + i, g))] + cast_specs,
        out_shape=[jax.ShapeDtypeStruct((m, heads * dh), BF16)] + cast_shapes,
        scratch_shapes=[pltpu.VMEM((hp, length, 2 * dh), BF16),
                        pltpu.VMEM((hp, blk, LANES), F32),
                        pltpu.VMEM((hp, blk, 2 * dh), F32)],
        compiler_params=_cparams("parallel", "parallel", "arbitrary"),
        name="fox_attention",
    )(qkv, qkv, qkv, fcum_t, *cast_weights)
    return outs[0], outs[1:]


def _merge_body(ys_ref, ya_ref, ws_ref, wa_ref, gs_ref, ga_ref, cast_in, o_ref, cast_out):
    cast_out[...] = cast_in[...].astype(BF16)
    ps = jnp.dot(ys_ref[...], ws_ref[...], preferred_element_type=F32)
    pa = jnp.dot(ya_ref[...], wa_ref[...], preferred_element_type=F32)
    o_ref[...] = (gs_ref[...].astype(F32) * ps + ga_ref[...].astype(F32) * pa).astype(o_ref.dtype)


def _merge(y_ssd, y_att, w_ssd, w_att, gates, cast_weight, *, tm=1024, tn=256):
    m, ks = y_ssd.shape
    ka = y_att.shape[1]
    n = w_ssd.shape[1]
    tm, tn = _tile(m, tm), _tile(n, tn)
    nj = n // tn
    once = pl.Buffered(1)
    cast_specs, cast_shapes = _cast_slices([cast_weight], (m // tm) * nj, lambda i, j: i * nj + j)
    return pl.pallas_call(
        _merge_body,
        grid=(m // tm, nj),
        in_specs=[
            pl.BlockSpec((tm, ks), lambda i, j: (i, 0), pipeline_mode=once),
            pl.BlockSpec((tm, ka), lambda i, j: (i, 0), pipeline_mode=once),
            pl.BlockSpec((ks, tn), lambda i, j: (0, j)),
            pl.BlockSpec((ka, tn), lambda i, j: (0, j)),
            pl.BlockSpec((tm, tn), lambda i, j: (i, j)),
            pl.BlockSpec((tm, tn), lambda i, j: (i, nj + j)),
        ] + cast_specs,
        out_specs=[pl.BlockSpec((tm, tn), lambda i, j: (i, j))] + cast_specs,
        out_shape=[jax.ShapeDtypeStruct((m, n), BF16)] + cast_shapes,
        compiler_params=_cparams("parallel", "arbitrary"),
        name="merge_proj",
    )(y_ssd, y_att, w_ssd, w_att, gates, gates, cast_weight)


def _mm_residual_body(a_ref, w_ref, r_ref, o_ref, *, alpha):
    acc = jnp.dot(a_ref[...], w_ref[...], preferred_element_type=F32)
    o_ref[...] = alpha * r_ref[...] + acc


def _matmul_residual(a, w, resid, alpha, *, tm, tn, lhs_buffers=2, name):
    m, k = a.shape
    n = w.shape[1]
    tm, tn = _tile(m, tm), _tile(n, tn)
    return pl.pallas_call(
        functools.partial(_mm_residual_body, alpha=alpha),
        grid=(m // tm, n // tn),
        in_specs=[
            pl.BlockSpec((tm, k), lambda i, j: (i, 0), pipeline_mode=pl.Buffered(lhs_buffers)),
            pl.BlockSpec((k, tn), lambda i, j: (0, j)),
            pl.BlockSpec((tm, tn), lambda i, j: (i, j)),
        ],
        out_specs=pl.BlockSpec((tm, tn), lambda i, j: (i, j)),
        out_shape=jax.ShapeDtypeStruct((m, n), F32),
        compiler_params=_cparams("parallel", "arbitrary"),
        name=name,
    )(a, w, resid)


def _ln_body(x_ref, g_ref, b_ref, *o_refs):
    x = x_ref[...]
    mu = jnp.mean(x, axis=-1, keepdims=True)
    xc = x - mu
    var = jnp.mean(xc * xc, axis=-1, keepdims=True)
    out = xc * lax.rsqrt(var + LN_EPS) * g_ref[...] + b_ref[...]
    for o_ref in o_refs:
        o_ref[...] = out.astype(o_ref.dtype)


def _layer_norm(x, gain, bias, out_dtypes, *, tm=256, name):
    m, d = x.shape
    tm = _tile(m, tm, SUBLANES)
    outs = pl.pallas_call(
        _ln_body,
        grid=(m // tm,),
        in_specs=[pl.BlockSpec((tm, d), lambda i: (i, 0)),
                  pl.BlockSpec((1, d), lambda i: (0, 0)),
                  pl.BlockSpec((1, d), lambda i: (0, 0))],
        out_specs=[pl.BlockSpec((tm, d), lambda i: (i, 0)) for _ in out_dtypes],
        out_shape=[jax.ShapeDtypeStruct((m, d), dt) for dt in out_dtypes],
        compiler_params=_cparams("parallel"),
        name=name,
    )(x, gain, bias)
    return outs


def _ffn_up_body(a_ref, halo_ref, wv_ref, wg_ref, cwv_ref, cwg_ref, cbv_ref, cbg_ref, cast_in, o_ref, cast_out,
                 wbf_ref, *, kw, tiles_per_seq):
    i = pl.program_id(1)
    pad = BF16_SUBLANES
    cast_out[...] = cast_in[...].astype(BF16)

    @pl.when(i == 0)
    def _():
        wbf_ref[0] = wv_ref[...].astype(BF16)
        wbf_ref[1] = wg_ref[...].astype(BF16)

    a = a_ref[...]
    halo = halo_ref[...]
    keep = (i % tiles_per_seq != 0).astype(F32)

    def conv(c, cw_ref, cb_ref):
        w = wbf_ref[c]
        u = jnp.dot(a, w, preferred_element_type=F32)
        uh = jnp.dot(halo, w, preferred_element_type=F32) * keep
        cw = cw_ref[...]
        cb = cb_ref[...]
        out = cb + cw[kw - 1:kw, :] * u
        head_src = jnp.concatenate([uh, u[0:pad, :]], axis=0)
        head = cb + cw[kw - 1:kw, :] * head_src[pad:2 * pad, :]
        for k in range(kw - 1):
            shift = kw - 1 - k
            out = out + cw[k:k + 1, :] * pltpu.roll(u, shift, axis=0)
            head = head + cw[k:k + 1, :] * head_src[pad - shift:2 * pad - shift, :]
        return out, head

    gate, gate_head = conv(1, cwg_ref, cbg_ref)
    gate, gate_head = _silu(gate), _silu(gate_head)
    val, val_head = conv(0, cwv_ref, cbv_ref)
    o_ref[...] = (gate * val).astype(o_ref.dtype)
    o_ref[0:pad, :] = (gate_head * val_head).astype(o_ref.dtype)


def _ffn_up(h, w_up, conv_w, conv_b, cast_weight, *, length, d_ff, tm=1024, tn=256):
    m, k = h.shape
    tm = _tile(min(m, length), tm, BF16_SUBLANES)
    assert length % tm == 0
    tn = _tile(d_ff, tn)
    nj = d_ff // tn
    ni = m // tm
    kw = conv_w.shape[0]
    hb = tm // BF16_SUBLANES
    cast_specs, cast_shapes = _cast_slices([cast_weight], nj * ni, lambda j, i: j * ni + i)
    return pl.pallas_call(
        functools.partial(_ffn_up_body, kw=kw, tiles_per_seq=length // tm),
        grid=(nj, m // tm),
        in_specs=[
            pl.BlockSpec((tm, k), lambda j, i: (i, 0)),
            pl.BlockSpec((BF16_SUBLANES, k), lambda j, i: (jnp.maximum(i * hb - 1, 0), 0)),
            pl.BlockSpec((k, tn), lambda j, i: (0, j)),
            pl.BlockSpec((k, tn), lambda j, i: (0, nj + j)),
            pl.BlockSpec((kw, tn), lambda j, i: (0, j)),
            pl.BlockSpec((kw, tn), lambda j, i: (0, nj + j)),
            pl.BlockSpec((1, tn), lambda j, i: (0, j)),
            pl.BlockSpec((1, tn), lambda j, i: (0, nj + j)),
        ] + cast_specs,
        out_specs=[pl.BlockSpec((tm, tn), lambda j, i: (i, j))] + cast_specs,
        out_shape=[jax.ShapeDtypeStruct((m, d_ff), BF16)] + cast_shapes,
        scratch_shapes=[pltpu.VMEM((2, k, tn), BF16)],
        compiler_params=_cparams("parallel", "arbitrary"),
        name="ffn_up_conv_act",
    )(h, h, w_up, w_up, conv_w, conv_w, conv_b, conv_b, cast_weight)


def _layer(h, p, *, batch, length, alpha):
    m, d = h.shape
    d_inner = p["ssd_norm_w"].shape[-1]
    conv_dim = p["ssd_conv_b"].shape[-1]
    ssd_heads = p["ssd_dt_bias"].shape[-1]
    fox_heads = p["fox_f_bias"].shape[-1]
    d_att = fox_heads * FOX_HEAD_DIM
    d_ff = p["w_down"].shape[0]
    groups = (conv_dim - d_inner) // (2 * SSD_STATE)
    assert ssd_heads <= LANES and fox_heads <= LANES

    o_z, o_xbc = 0, d_inner
    o_dt = o_xbc + conv_dim
    o_q = o_dt + ssd_heads
    o_f = o_q + 3 * d_att
    o_g = o_f + fox_heads
    wt_in = p["w_in"].T.astype(BF16)
    zeros = lambda n: jnp.zeros((n, d), BF16)
    wt_small = jnp.concatenate([wt_in[o_dt:o_q], zeros(LANES - ssd_heads),
                                wt_in[o_f:o_g], zeros(LANES - fox_heads)], axis=0)

    h_bf = h.astype(BF16)
    zx = _proj(h_bf, wt_in, F32, row0=o_z, n=o_dt, name="in_proj_zx")
    qkv = _proj(h_bf, wt_in, BF16, row0=o_q, n=3 * d_att, scaled_cols=d_att, scale=1.0 / math.sqrt(FOX_HEAD_DIM),
                name="in_proj_qkv")
    gates = _proj(h_bf, wt_in, F32, row0=o_g, n=2 * d, bias=p["gate_bias"].reshape(1, 2 * d), name="in_proj_gates")
    small = _proj(h_bf, wt_small, F32, name="in_proj_small")

    pad_row = lambda v, n: jnp.pad(v.reshape(1, -1).astype(F32), ((0, 0), (0, n - v.shape[-1])))
    acs, dt_t, acs_t, fcum = _head_prep(small, pad_row(p["ssd_dt_bias"], LANES), pad_row(p["ssd_a_log"], LANES),
                                        pad_row(p["fox_f_bias"], LANES),
                                        batch=batch, length=length, fox_heads=fox_heads)
    y_ssd = _ssd(zx, acs, dt_t, acs_t, p["ssd_conv_w"], p["ssd_conv_b"].reshape(1, -1),
                 jnp.repeat(p["ssd_d"].astype(F32), SSD_HEAD_DIM).reshape(1, -1),
                 p["ssd_norm_w"].reshape(1, -1),
                 batch=batch, length=length, d_inner=d_inner, groups=groups)
    y_att, (w_ssd_bf, w_att_bf) = _attention(qkv, fcum, [p["w_proj_ssd"], p["w_proj_att"]],
                                             batch=batch, length=length, heads=fox_heads)

    merged, w_out_bf = _merge(y_ssd, y_att, w_ssd_bf, w_att_bf, gates, p["w_out"])
    pre1 = _matmul_residual(merged, w_out_bf, h, alpha, tm=1024, tn=512, name="out_proj_residual")
    h1, h1_bf = _layer_norm(pre1, p["ln1_g"].reshape(1, -1), p["ln1_b"].reshape(1, -1), (F32, BF16), name="layer_norm_1")

    act, w_down_bf = _ffn_up(h1_bf, p["w_up"], p["ffn_conv_w"], p["ffn_conv_b"].reshape(1, -1), p["w_down"],
                             length=length, d_ff=d_ff)
    pre2 = _matmul_residual(act, w_down_bf, h1, alpha, tm=1024, tn=256, lhs_buffers=1,
                            name="ffn_down_residual")
    (out,) = _layer_norm(pre2, p["ln2_g"].reshape(1, -1), p["ln2_b"].reshape(1, -1), (F32,), name="layer_norm_2")
    return out


_PARAM_NAMES = ("w_in", "ssd_conv_w", "ssd_conv_b", "ssd_dt_bias", "ssd_a_log", "ssd_d", "ssd_norm_w",
                "fox_f_bias", "gate_bias", "w_proj_ssd", "w_proj_att", "w_out", "ln1_g", "ln1_b",
                "w_up", "ffn_conv_w", "ffn_conv_b", "w_down", "ln2_g", "ln2_b")


def kernel(x, w_in, ssd_conv_w, ssd_conv_b, ssd_dt_bias, ssd_a_log, ssd_d, ssd_norm_w, fox_f_bias, gate_bias,
           w_proj_ssd, w_proj_att, w_out, ln1_g, ln1_b, w_up, ffn_conv_w, ffn_conv_b, w_down, ln2_g, ln2_b):
    params = (w_in, ssd_conv_w, ssd_conv_b, ssd_dt_bias, ssd_a_log, ssd_d, ssd_norm_w, fox_f_bias, gate_bias,
              w_proj_ssd, w_proj_att, w_out, ln1_g, ln1_b, w_up, ffn_conv_w, ffn_conv_b, w_down, ln2_g, ln2_b)
    batch, length, d = x.shape
    depth = w_in.shape[0]
    alpha = (2.0 * depth) ** 0.25
    h = x.reshape(batch * length, d)
    for layer in range(depth):
        p = {name: arr[layer] for name, arr in zip(_PARAM_NAMES, params)}
        h = _layer(h, p, batch=batch, length=length, alpha=alpha)
    return h.reshape(batch, length, d)
```

```python
import functools
import math

import jax
import jax.numpy as jnp
from jax import lax
from jax.experimental import pallas as pl
from jax.experimental.pallas import tpu as pltpu

F32 = jnp.float32
BF16 = jnp.bfloat16

SSD_HEAD_DIM = 64
SSD_STATE = 128
FOX_HEAD_DIM = 128
LN_EPS = 1e-5
RMS_EPS = 1e-5

LANES = 128
SUBLANES = 8
BF16_SUBLANES = 16
VMEM_LIMIT_BYTES = 56 * 1024 * 1024

SSD_CHUNK = 128
ATT_BLOCK = 512
ATT_HEADS_PER_STEP = 4


def _cparams(*sem):
    return pltpu.CompilerParams(dimension_semantics=sem, vmem_limit_bytes=VMEM_LIMIT_BYTES)


def _tile(n, pref, quantum=LANES):
    if n <= pref:
        return n
    t = (pref // quantum) * quantum
    while t > quantum and n % t:
        t -= quantum
    assert n % t == 0, (n, pref, quantum)
    return t


def _cast_slices(weights, steps, linear_step):
    specs, shapes = [], []
    for w in weights:
        rows, cols = w.shape
        assert rows % steps == 0 and (rows // steps) % BF16_SUBLANES == 0, (w.shape, steps)
        specs.append(pl.BlockSpec((rows // steps, cols), lambda *ids: (linear_step(*ids), 0)))
        shapes.append(jax.ShapeDtypeStruct((rows, cols), BF16))
    return specs, shapes


def _softplus(x):
    return jnp.maximum(x, 0.0) + jnp.log1p(jnp.exp(-jnp.abs(x)))


def _log_sigmoid(x):
    return jnp.minimum(x, 0.0) - jnp.log1p(jnp.exp(-jnp.abs(x)))


def _silu(x):
    h = 0.5 * x
    return h + h * jnp.tanh(h)


def _proj_body(a_ref, wt_ref, *rest, scaled_tiles, scale, sigmoid, n_cast):
    if sigmoid:
        b_ref, rest = rest[0], rest[1:]
    cast_in, o_ref, cast_out = rest[:n_cast], rest[n_cast], rest[n_cast + 1:]
    for src, dst in zip(cast_in, cast_out):
        dst[...] = src[...].astype(BF16)
    acc = lax.dot_general(a_ref[...], wt_ref[...], (((1,), (1,)), ((), ())), preferred_element_type=F32)
    if scaled_tiles:
        acc = acc * jnp.where(pl.program_id(1) < scaled_tiles, scale, 1.0)
    if sigmoid:
        acc = jax.nn.sigmoid(acc + b_ref[...])
    o_ref[...] = acc.astype(o_ref.dtype)


def _cast_body(src_ref, dst_ref):
    dst_ref[...] = src_ref[...].astype(dst_ref.dtype)


def _cast_rows(w, first, rows, *, block_rows=640):
    assert first % SUBLANES == 0
    block_rows = _tile(rows, block_rows, BF16_SUBLANES)
    cols = w.shape[1]
    return pl.pallas_call(
        _cast_body,
        grid=(rows // block_rows,),
        in_specs=[pl.BlockSpec((pl.Element(block_rows), pl.Element(cols)),
                               lambda i: (pl.multiple_of(first + i * block_rows, SUBLANES), 0))],
        out_specs=pl.BlockSpec((block_rows, cols), lambda i: (i, 0)),
        out_shape=jax.ShapeDtypeStruct((rows, cols), BF16),
        compiler_params=_cparams("parallel"),
        name="weight_cast",
    )(w)


def _proj(a, wt, out_dtype, *, row0=0, n=None, bias=None, scaled_cols=0, scale=1.0, casts=(), tm=1024, tn=1024,
          lhs_buffers=2, name):
    m, k = a.shape
    n = wt.shape[0] if n is None else n
    tm, tn = _tile(m, tm), _tile(n, tn)
    assert scaled_cols % tn == 0 and row0 % BF16_SUBLANES == 0
    nj = n // tn
    steps = (m // tm) * nj
    in_specs = [pl.BlockSpec((tm, k), lambda i, j: (i, 0), pipeline_mode=pl.Buffered(lhs_buffers)),
                pl.BlockSpec((pl.Element(tn), pl.Element(k)),
                             lambda i, j: (pl.multiple_of(row0 + j * tn, BF16_SUBLANES), 0))]
    args = [a, wt]
    if bias is not None:
        in_specs.append(pl.BlockSpec((1, tn), lambda i, j: (0, j)))
        args.append(bias)
    out_specs = [pl.BlockSpec((tm, tn), lambda i, j: (i, j))]
    out_shape = [jax.ShapeDtypeStruct((m, n), out_dtype)]
    for w, first, rows in casts:
        assert first % SUBLANES == 0
        blk = next(b for b in range(BF16_SUBLANES, rows + 1, BF16_SUBLANES) if rows % b == 0 and rows // b <= steps)
        nblk = rows // blk
        slot = lambda i, j, nblk=nblk: jnp.minimum(i * nj + j, nblk - 1)
        in_specs.append(pl.BlockSpec((pl.Element(blk), pl.Element(w.shape[1])),
                                     lambda i, j, first=first, blk=blk, slot=slot:
                                     (pl.multiple_of(first + slot(i, j) * blk, SUBLANES), 0)))
        args.append(w)
        out_specs.append(pl.BlockSpec((blk, w.shape[1]), lambda i, j, slot=slot: (slot(i, j), 0)))
        out_shape.append(jax.ShapeDtypeStruct((rows, w.shape[1]), BF16))
    outs = pl.pallas_call(
        functools.partial(_proj_body, scaled_tiles=scaled_cols // tn, scale=scale, sigmoid=bias is not None,
                          n_cast=len(casts)),
        grid=(m // tm, nj),
        in_specs=in_specs,
        out_specs=out_specs,
        out_shape=out_shape,
        compiler_params=_cparams("parallel", "arbitrary"),
        name=name,
    )(*args)
    return outs[0] if not casts else outs


def _head_prep_body(dt_ref, f_ref, dtb_ref, alog_ref, fb_ref, acs_ref, dtt_ref, acst_ref, fcum_ref, carry_ref, *,
                    fox_heads):
    @pl.when(pl.program_id(1) == 0)
    def _():
        carry_ref[...] = jnp.zeros_like(carry_ref)

    q = dt_ref.shape[0]
    row = lax.broadcasted_iota(jnp.int32, (q, q), 0)
    col = lax.broadcasted_iota(jnp.int32, (q, q), 1)
    tri = (row >= col).astype(F32)
    dtv = _softplus(dt_ref[...] + dtb_ref[...])
    da = dtv * (-jnp.exp(alog_ref[...]))
    acs = jnp.dot(tri, da, preferred_element_type=F32, precision=lax.Precision.HIGHEST)
    acs_ref[...] = acs
    dtt_ref[...] = dtv.T
    acst_ref[...] = acs.T
    logf = _log_sigmoid(f_ref[...] + fb_ref[...])
    cs = jnp.dot(tri, logf, preferred_element_type=F32, precision=lax.Precision.HIGHEST) + carry_ref[...]
    carry_ref[...] = cs[q - 1:q, :]
    fcum_ref[...] = cs.T[0:fox_heads, :]


def _head_prep(small, dt_bias, a_log, f_bias, *, batch, length, fox_heads):
    m = small.shape[0]
    q = SSD_CHUNK
    nc = length // q
    nat = lambda lane_blk: pl.BlockSpec((q, LANES), lambda b, c: (b * nc + c, lane_blk))
    tr = pl.BlockSpec((LANES, q), lambda b, c: (0, b * nc + c))
    vec = pl.BlockSpec((1, LANES), lambda b, c: (0, 0))
    return pl.pallas_call(
        functools.partial(_head_prep_body, fox_heads=fox_heads),
        grid=(batch, nc),
        in_specs=[nat(0), nat(1), vec, vec, vec],
        out_specs=[nat(0), tr, tr, pl.BlockSpec((None, fox_heads, q), lambda b, c: (b, 0, c))],
        out_shape=[jax.ShapeDtypeStruct((m, LANES), F32), jax.ShapeDtypeStruct((LANES, m), F32),
                   jax.ShapeDtypeStruct((LANES, m), F32), jax.ShapeDtypeStruct((batch, fox_heads, length), F32)],
        scratch_shapes=[pltpu.VMEM((1, LANES), F32)],
        compiler_params=_cparams("parallel", "arbitrary"),
        name="head_prep",
    )(small, small, dt_bias, a_log, f_bias)


def _ssd_body(z_ref, xs_ref, b_ref, c_ref, acs_ref, dtt_ref, acst_ref, px_ref, pb_ref, pc_ref, y_ref,
              h_ref, tail_ref, ybuf_ref, *, q, r, kw):
    g = pl.program_id(1)
    c = pl.program_id(2)
    gw = r * SSD_HEAD_DIM
    n = SSD_STATE

    @pl.when(c == 0)
    def _():
        h_ref[...] = jnp.zeros_like(h_ref)
        tail_ref[...] = jnp.zeros_like(tail_ref)

    def conv_silu(cur_ref, lo, hi, p_ref):
        cur = cur_ref[...].astype(F32)
        p = p_ref[...]
        bias = p[kw:kw + 1, :]
        out = bias + p[kw - 1:kw, :] * cur
        head_src = jnp.concatenate([tail_ref[:, lo:hi], cur[0:SUBLANES, :]], axis=0)
        head = bias + p[kw - 1:kw, :] * head_src[SUBLANES:2 * SUBLANES, :]
        for k in range(kw - 1):
            shift = kw - 1 - k
            out = out + p[k:k + 1, :] * pltpu.roll(cur, shift, axis=0)
            head = head + p[k:k + 1, :] * head_src[SUBLANES - shift:2 * SUBLANES - shift, :]
        tail_ref[:, lo:hi] = cur[q - SUBLANES:q, :]
        return _silu(jnp.concatenate([head, out[SUBLANES:, :]], axis=0))

    xs = conv_silu(xs_ref, 0, gw, px_ref)
    bm = conv_silu(b_ref, gw, gw + n, pb_ref)
    cm = conv_silu(c_ref, gw + n, gw + 2 * n, pc_ref)

    row = lax.broadcasted_iota(jnp.int32, (q, q), 0)
    col = lax.broadcasted_iota(jnp.int32, (q, q), 1)
    tri = row >= col
    acs_g = pltpu.roll(acs_ref[...], (LANES - g * r) % LANES, axis=1)
    dt_t = dtt_ref[...]
    acs_t = acst_ref[...]

    cm_bf = cm.astype(BF16)
    cb = lax.dot_general(cm_bf, bm.astype(BF16), (((1,), (1,)), ((), ())), preferred_element_type=F32)
    bm_t = bm.T

    lane = lax.broadcasted_iota(jnp.int32, (q, LANES), 1)
    lo_half = lane < SSD_HEAD_DIM
    lo_half_row = lo_half[0:1, :]
    pairs = range(r // 2)
    pair_cols = [slice(j * LANES, (j + 1) * LANES) for j in pairs]
    y_off = jnp.dot(cm_bf, h_ref[...].astype(BF16), preferred_element_type=F32)

    lhs_y, lhs_s, rhs, e_pair, cd_pair = [], [], [], [], []
    for j in pairs:
        xs_p = xs[:, pair_cols[j]]
        rhs.append(jnp.concatenate([jnp.where(lo_half, xs_p, 0.0).astype(BF16),
                                    jnp.where(lo_half, 0.0, xs_p).astype(BF16)], axis=0))
        m_parts, bw_parts, e_cols, cd = [], [], [], []
        for hd in (2 * j, 2 * j + 1):
            a_col = acs_g[:, hd:hd + 1]
            a_row = acs_t[hd:hd + 1, :]
            dt_row = dt_t[hd:hd + 1, :]
            a_last = acs_g[q - 1:q, hd:hd + 1]
            lmat = jnp.exp(jnp.where(tri, a_col - a_row, -jnp.inf))
            m_parts.append(cb * lmat * dt_row)
            bw_parts.append(bm_t * (jnp.exp(a_last - a_row) * dt_row))
            e_cols.append(jnp.exp(a_col))
            cd.append(jnp.exp(a_last))
        lhs_y.append(jnp.concatenate(m_parts, axis=1).astype(BF16))
        lhs_s.append(jnp.concatenate(bw_parts, axis=1).astype(BF16))
        e_pair.append(jnp.where(lo_half, e_cols[0], e_cols[1]))
        cd_pair.append(jnp.where(lo_half_row, cd[0], cd[1]))
    y_diag = [jnp.dot(lhs_y[j], rhs[j], preferred_element_type=F32) for j in pairs]
    s_new = [jnp.dot(lhs_s[j], rhs[j], preferred_element_type=F32) for j in pairs]
    ssq = jnp.zeros((q, 1), F32)
    for j in pairs:
        cols = pair_cols[j]
        h_ref[:, cols] = cd_pair[j] * h_ref[:, cols] + s_new[j]
        y = y_diag[j] + e_pair[j] * y_off[:, cols] + px_ref[kw + 1:kw + 2, cols] * xs[:, cols]
        y = y * _silu(z_ref[:, cols].astype(F32))
        ybuf_ref[:, cols] = y
        ssq = ssq + jnp.sum(y * y, axis=1, keepdims=True)
    inv = lax.rsqrt(ssq / gw + RMS_EPS)
    y_ref[...] = (ybuf_ref[...] * inv * px_ref[kw + 2:kw + 3, :]).astype(y_ref.dtype)


def _ssd(zx, acs, dt_t, acs_t, conv_w, conv_b, d_cols, norm_w, *, batch, length, d_inner, groups):
    m = zx.shape[0]
    q = SSD_CHUNK
    assert length % q == 0
    nc = length // q
    gw = d_inner // groups
    r = gw // SSD_HEAD_DIM
    assert r % 2 == 0 and r % SUBLANES == 0 and r * groups <= LANES
    kw = conv_w.shape[0]
    n = SSD_STATE
    conv_dim = conv_w.shape[1]
    rows = jnp.concatenate([conv_w, conv_b], axis=0)
    extra = jnp.zeros((2, conv_dim), F32).at[0, :d_inner].set(d_cols[0]).at[1, :d_inner].set(norm_w[0])
    params = jnp.concatenate([rows, extra], axis=0)
    np_rows = params.shape[0]
    zblk = d_inner // gw
    bblk = 2 * d_inner // n
    cblk_w = d_inner // n
    row = lambda b, g, c: b * nc + c
    in_specs = [
        pl.BlockSpec((q, gw), lambda b, g, c: (row(b, g, c), g)),
        pl.BlockSpec((q, gw), lambda b, g, c: (row(b, g, c), zblk + g)),
        pl.BlockSpec((q, n), lambda b, g, c: (row(b, g, c), bblk + g)),
        pl.BlockSpec((q, n), lambda b, g, c: (row(b, g, c), bblk + groups + g)),
        pl.BlockSpec((q, LANES), lambda b, g, c: (row(b, g, c), 0)),
        pl.BlockSpec((r, q), lambda b, g, c: (g, row(b, g, c))),
        pl.BlockSpec((r, q), lambda b, g, c: (g, row(b, g, c))),
        pl.BlockSpec((np_rows, gw), lambda b, g, c: (0, g)),
        pl.BlockSpec((np_rows, n), lambda b, g, c: (0, cblk_w + g)),
        pl.BlockSpec((np_rows, n), lambda b, g, c: (0, cblk_w + groups + g)),
    ]
    return pl.pallas_call(
        functools.partial(_ssd_body, q=q, r=r, kw=kw),
        grid=(batch, groups, nc),
        in_specs=in_specs,
        out_specs=pl.BlockSpec((q, gw), lambda b, g, c: (row(b, g, c), g)),
        out_shape=jax.ShapeDtypeStruct((m, d_inner), BF16),
        scratch_shapes=[
            pltpu.VMEM((n, gw), F32),
            pltpu.VMEM((SUBLANES, gw + 2 * n), F32),
            pltpu.VMEM((q, gw), F32),
        ],
        compiler_params=_cparams("parallel", "parallel", "arbitrary"),
        name="ssd_scan",
    )(zx, zx, zx, zx, acs, dt_t, acs_t, params, params, params)


def _attn_body(q_ref, k_ref, v_ref, f_ref, *rest, blk, hp, n_cast):
    cast_in, o_ref, cast_out = rest[:n_cast], rest[n_cast], rest[n_cast + 1:2 * n_cast + 1]
    vaug_ref, m_ref, acc_ref = rest[2 * n_cast + 1:]
    for src, dst in zip(cast_in, cast_out):
        dst[...] = src[...].astype(BF16)
    qi = pl.program_id(2)
    dh = FOX_HEAD_DIM
    length = k_ref.shape[0]

    @pl.when(qi == 0)
    def _():
        for h in range(hp):
            vaug_ref[h, :, 0:dh] = v_ref[:, h * dh:(h + 1) * dh]
            vaug_ref[h, :, dh:2 * dh] = jnp.ones((length, dh), BF16)

    m_ref[...] = jnp.full_like(m_ref, -jnp.inf)
    acc_ref[...] = jnp.zeros_like(acc_ref)

    def step(j, masked):
        start = pl.multiple_of(j * blk, blk)

        def scores(h):
            qv = q_ref[:, h * dh:(h + 1) * dh]
            kj = k_ref[pl.ds(start, blk), h * dh:(h + 1) * dh]
            return lax.dot_general(qv, kj, (((1,), (1,)), ((), ())), preferred_element_type=F32)

        if masked:
            causal = (lax.broadcasted_iota(jnp.int32, (blk, blk), 0)
                      >= lax.broadcasted_iota(jnp.int32, (blk, blk), 1))

        def softmax_part(h, s):
            s = s - f_ref[h, pl.ds(j, 1), :]
            if masked:
                s = jnp.where(causal, s, -jnp.inf)
            m_old = m_ref[h]
            m_new = jnp.maximum(m_old, jnp.max(s, axis=1, keepdims=True))
            m_ref[h] = m_new
            return jnp.exp(s - jnp.tile(m_new, (1, blk // LANES))).astype(BF16), jnp.exp(m_old - m_new)

        def accumulate(h, p, alpha):
            pv = jnp.dot(p, vaug_ref[h, pl.ds(start, blk), :], preferred_element_type=F32)
            acc_ref[h] = jnp.tile(alpha, (1, 2)) * acc_ref[h] + pv

        s_cur = scores(0)
        pending = None
        for h in range(hp):
            s_nxt = scores(h + 1) if h + 1 < hp else None
            p, alpha = softmax_part(h, s_cur)
            if pending is not None:
                accumulate(*pending)
            pending = (h, p, alpha)
            s_cur = s_nxt
        accumulate(*pending)

    def loop_body(j, carry):
        step(j, False)
        return carry

    lax.fori_loop(0, qi, loop_body, 0)
    step(qi, True)
    for h in range(hp):
        acc = acc_ref[h]
        o_ref[:, h * dh:(h + 1) * dh] = (acc[:, 0:dh] / acc[:, dh:2 * dh]).astype(o_ref.dtype)


def _attention(qkv, fcum_t, cast_weights, *, batch, length, heads):
    m = qkv.shape[0]
    dh = FOX_HEAD_DIM
    hp = ATT_HEADS_PER_STEP
    assert dh == LANES and heads % hp == 0
    blk = _tile(length, ATT_BLOCK)
    nq = length // blk
    ng = heads // hp
    fcum_t = fcum_t.reshape(batch * ng, hp, nq, blk)
    cast_specs, cast_shapes = _cast_slices(cast_weights, batch * ng * nq, lambda b, g, i: (b * ng + g) * nq + i)
    outs = pl.pallas_call(
        functools.partial(_attn_body, blk=blk, hp=hp, n_cast=len(cast_weights)),
        grid=(batch, ng, nq),
        in_specs=[
            pl.BlockSpec((blk, hp * dh), lambda b, g, i: (b * nq + i, g)),
            pl.BlockSpec((length, hp * dh), lambda b, g, i: (b, ng + g)),
            pl.BlockSpec((length, hp * dh), lambda b, g, i: (b, 2 * ng + g)),
            pl.BlockSpec((None, hp, nq, blk), lambda b, g, i: (b * ng + g, 0, 0, 0)),
        ] + cast_specs,
        out_specs=[pl.BlockSpec((blk, hp * dh), lambda b, g, i: (b * nq + i, g))] + cast_specs,
        out_shape=[jax.ShapeDtypeStruct((m, heads * dh), BF16)] + cast_shapes,
        scratch_shapes=[pltpu.VMEM((hp, length, 2 * dh), BF16),
                        pltpu.VMEM((hp, blk, LANES), F32),
                        pltpu.VMEM((hp, blk, 2 * dh), F32)],
        compiler_params=_cparams("parallel", "parallel", "arbitrary"),
        name="fox_attention",
    )(qkv, qkv, qkv, fcum_t, *cast_weights)
    return outs[0], outs[1:]


def _merge_body(ys_ref, ya_ref, ws_ref, wa_ref, gs_ref, ga_ref, cast_in, o_ref, cast_out):
    cast_out[...] = cast_in[...].astype(BF16)
    ps = jnp.dot(ys_ref[...], ws_ref[...], preferred_element_type=F32)
    pa = jnp.dot(ya_ref[...], wa_ref[...], preferred_element_type=F32)
    o_ref[...] = (gs_ref[...].astype(F32) * ps + ga_ref[...].astype(F32) * pa).astype(o_ref.dtype)


def _merge(y_ssd, y_att, w_ssd, w_att, gates, cast_weight, *, tm=1024, tn=256):
    m, ks = y_ssd.shape
    ka = y_att.shape[1]
    n = w_ssd.shape[1]
    tm, tn = _tile(m, tm), _tile(n, tn)
    nj = n // tn
    once = pl.Buffered(1)
    cast_specs, cast_shapes = _cast_slices([cast_weight], (m // tm) * nj, lambda i, j: i * nj + j)
    return pl.pallas_call(
        _merge_body,
        grid=(m // tm, nj),
        in_specs=[
            pl.BlockSpec((tm, ks), lambda i, j: (i, 0), pipeline_mode=once),
            pl.BlockSpec((tm, ka), lambda i, j: (i, 0), pipeline_mode=once),
            pl.BlockSpec((ks, tn), lambda i, j: (0, j)),
            pl.BlockSpec((ka, tn), lambda i, j: (0, j)),
            pl.BlockSpec((tm, tn), lambda i, j: (i, j)),
            pl.BlockSpec((tm, tn), lambda i, j: (i, nj + j)),
        ] + cast_specs,
        out_specs=[pl.BlockSpec((tm, tn), lambda i, j: (i, j))] + cast_specs,
        out_shape=[jax.ShapeDtypeStruct((m, n), BF16)] + cast_shapes,
        compiler_params=_cparams("parallel", "arbitrary"),
        name="merge_proj",
    )(y_ssd, y_att, w_ssd, w_att, gates, gates, cast_weight)


def _mm_residual_body(a_ref, w_ref, r_ref, o_ref, *, alpha):
    acc = jnp.dot(a_ref[...], w_ref[...], preferred_element_type=F32)
    o_ref[...] = alpha * r_ref[...] + acc


def _matmul_residual(a, w, resid, alpha, *, tm, tn, lhs_buffers=2, name):
    m, k = a.shape
    n = w.shape[1]
    tm, tn = _tile(m, tm), _tile(n, tn)
    return pl.pallas_call(
        functools.partial(_mm_residual_body, alpha=alpha),
        grid=(m // tm, n // tn),
        in_specs=[
            pl.BlockSpec((tm, k), lambda i, j: (i, 0), pipeline_mode=pl.Buffered(lhs_buffers)),
            pl.BlockSpec((k, tn), lambda i, j: (0, j)),
            pl.BlockSpec((tm, tn), lambda i, j: (i, j)),
        ],
        out_specs=pl.BlockSpec((tm, tn), lambda i, j: (i, j)),
        out_shape=jax.ShapeDtypeStruct((m, n), F32),
        compiler_params=_cparams("parallel", "arbitrary"),
        name=name,
    )(a, w, resid)


def _ln_body(x_ref, g_ref, b_ref, *o_refs):
    x = x_ref[...]
    mu = jnp.mean(x, axis=-1, keepdims=True)
    xc = x - mu
    var = jnp.mean(xc * xc, axis=-1, keepdims=True)
    out = xc * lax.rsqrt(var + LN_EPS) * g_ref[...] + b_ref[...]
    for o_ref in o_refs:
        o_ref[...] = out.astype(o_ref.dtype)


def _layer_norm(x, gain, bias, out_dtypes, *, tm=256, name):
    m, d = x.shape
    tm = _tile(m, tm, SUBLANES)
    outs = pl.pallas_call(
        _ln_body,
        grid=(m // tm,),
        in_specs=[pl.BlockSpec((tm, d), lambda i: (i, 0)),
                  pl.BlockSpec((1, d), lambda i: (0, 0)),
                  pl.BlockSpec((1, d), lambda i: (0, 0))],
        out_specs=[pl.BlockSpec((tm, d), lambda i: (i, 0)) for _ in out_dtypes],
        out_shape=[jax.ShapeDtypeStruct((m, d), dt) for dt in out_dtypes],
        compiler_params=_cparams("parallel"),
        name=name,
    )(x, gain, bias)
    return outs


def _ffn_up_body(a_ref, halo_ref, wv_ref, wg_ref, cwv_ref, cwg_ref, cbv_ref, cbg_ref, cast_in, o_ref, cast_out,
                 wbf_ref, *, kw, tiles_per_seq):
    i = pl.program_id(1)
    pad = BF16_SUBLANES
    cast_out[...] = cast_in[...].astype(BF16)

    @pl.when(i == 0)
    def _():
        wbf_ref[0] = wv_ref[...].astype(BF16)
        wbf_ref[1] = wg_ref[...].astype(BF16)

    a = a_ref[...]
    halo = halo_ref[...]
    keep = (i % tiles_per_seq != 0).astype(F32)

    def conv(c, cw_ref, cb_ref):
        w = wbf_ref[c]
        u = jnp.dot(a, w, preferred_element_type=F32)
        uh = jnp.dot(halo, w, preferred_element_type=F32) * keep
        cw = cw_ref[...]
        cb = cb_ref[...]
        out = cb + cw[kw - 1:kw, :] * u
        head_src = jnp.concatenate([uh, u[0:pad, :]], axis=0)
        head = cb + cw[kw - 1:kw, :] * head_src[pad:2 * pad, :]
        for k in range(kw - 1):
            shift = kw - 1 - k
            out = out + cw[k:k + 1, :] * pltpu.roll(u, shift, axis=0)
            head = head + cw[k:k + 1, :] * head_src[pad - shift:2 * pad - shift, :]
        return out, head

    gate, gate_head = conv(1, cwg_ref, cbg_ref)
    gate, gate_head = _silu(gate), _silu(gate_head)
    val, val_head = conv(0, cwv_ref, cbv_ref)
    o_ref[...] = (gate * val).astype(o_ref.dtype)
    o_ref[0:pad, :] = (gate_head * val_head).astype(o_ref.dtype)


def _ffn_up(h, w_up, conv_w, conv_b, cast_weight, *, length, d_ff, tm=1024, tn=256):
    m, k = h.shape
    tm = _tile(min(m, length), tm, BF16_SUBLANES)
    assert length % tm == 0
    tn = _tile(d_ff, tn)
    nj = d_ff // tn
    ni = m // tm
    kw = conv_w.shape[0]
    hb = tm // BF16_SUBLANES
    cast_specs, cast_shapes = _cast_slices([cast_weight], nj * ni, lambda j, i: j * ni + i)
    return pl.pallas_call(
        functools.partial(_ffn_up_body, kw=kw, tiles_per_seq=length // tm),
        grid=(nj, m // tm),
        in_specs=[
            pl.BlockSpec((tm, k), lambda j, i: (i, 0)),
            pl.BlockSpec((BF16_SUBLANES, k), lambda j, i: (jnp.maximum(i * hb - 1, 0), 0)),
            pl.BlockSpec((k, tn), lambda j, i: (0, j)),
            pl.BlockSpec((k, tn), lambda j, i: (0, nj + j)),
            pl.BlockSpec((kw, tn), lambda j, i: (0, j)),
            pl.BlockSpec((kw, tn), lambda j, i: (0, nj + j)),
            pl.BlockSpec((1, tn), lambda j, i: (0, j)),
            pl.BlockSpec((1, tn), lambda j, i: (0, nj + j)),
        ] + cast_specs,
        out_specs=[pl.BlockSpec((tm, tn), lambda j, i: (i, j))] + cast_specs,
        out_shape=[jax.ShapeDtypeStruct((m, d_ff), BF16)] + cast_shapes,
        scratch_shapes=[pltpu.VMEM((2, k, tn), BF16)],
        compiler_params=_cparams("parallel", "arbitrary"),
        name="ffn_up_conv_act",
    )(h, h, w_up, w_up, conv_w, conv_w, conv_b, conv_b, cast_weight)


def _layer(h, p, *, batch, length, alpha):
    m, d = h.shape
    d_inner = p["ssd_norm_w"].shape[-1]
    conv_dim = p["ssd_conv_b"].shape[-1]
    ssd_heads = p["ssd_dt_bias"].shape[-1]
    fox_heads = p["fox_f_bias"].shape[-1]
    d_att = fox_heads * FOX_HEAD_DIM
    d_ff = p["w_down"].shape[0]
    groups = (conv_dim - d_inner) // (2 * SSD_STATE)
    assert ssd_heads <= LANES and fox_heads <= LANES

    o_z, o_xbc = 0, d_inner
    o_dt = o_xbc + conv_dim
    o_q = o_dt + ssd_heads
    o_f = o_q + 3 * d_att
    o_g = o_f + fox_heads
    wt_f32 = p["w_in"].T
    wt_head = _cast_rows(wt_f32, 0, o_q)
    zeros = lambda n: jnp.zeros((n, d), BF16)
    wt_small = jnp.concatenate([wt_head[o_dt:o_q], zeros(LANES - ssd_heads),
                                _cast_rows(wt_f32, o_f, fox_heads), zeros(LANES - fox_heads)], axis=0)

    h_bf = h.astype(BF16)
    zx, wt_qkv, wt_gate = _proj(h_bf, wt_head, F32, row0=o_z, n=o_dt,
                                casts=[(wt_f32, o_q, 3 * d_att), (wt_f32, o_g, 2 * d)], lhs_buffers=1,
                                name="in_proj_zx")
    qkv = _proj(h_bf, wt_qkv, BF16, scaled_cols=d_att, scale=1.0 / math.sqrt(FOX_HEAD_DIM), name="in_proj_qkv")
    gates = _proj(h_bf, wt_gate, F32, bias=p["gate_bias"].reshape(1, 2 * d), name="in_proj_gates")
    small = _proj(h_bf, wt_small, F32, name="in_proj_small")

    pad_row = lambda v, n: jnp.pad(v.reshape(1, -1).astype(F32), ((0, 0), (0, n - v.shape[-1])))
    acs, dt_t, acs_t, fcum = _head_prep(small, pad_row(p["ssd_dt_bias"], LANES), pad_row(p["ssd_a_log"], LANES),
                                        pad_row(p["fox_f_bias"], LANES),
                                        batch=batch, length=length, fox_heads=fox_heads)
    y_ssd = _ssd(zx, acs, dt_t, acs_t, p["ssd_conv_w"], p["ssd_conv_b"].reshape(1, -1),
                 jnp.repeat(p["ssd_d"].astype(F32), SSD_HEAD_DIM).reshape(1, -1),
                 p["ssd_norm_w"].reshape(1, -1),
                 batch=batch, length=length, d_inner=d_inner, groups=groups)
    y_att, (w_ssd_bf, w_att_bf) = _attention(qkv, fcum, [p["w_proj_ssd"], p["w_proj_att"]],
                                             batch=batch, length=length, heads=fox_heads)

    merged, w_out_bf = _merge(y_ssd, y_att, w_ssd_bf, w_att_bf, gates, p["w_out"])
    pre1 = _matmul_residual(merged, w_out_bf, h, alpha, tm=1024, tn=512, name="out_proj_residual")
    h1, h1_bf = _layer_norm(pre1, p["ln1_g"].reshape(1, -1), p["ln1_b"].reshape(1, -1), (F32, BF16), name="layer_norm_1")

    act, w_down_bf = _ffn_up(h1_bf, p["w_up"], p["ffn_conv_w"], p["ffn_conv_b"].reshape(1, -1), p["w_down"],
                             length=length, d_ff=d_ff)
    pre2 = _matmul_residual(act, w_down_bf, h1, alpha, tm=1024, tn=256, lhs_buffers=1,
                            name="ffn_down_residual")
    (out,) = _layer_norm(pre2, p["ln2_g"].reshape(1, -1), p["ln2_b"].reshape(1, -1), (F32,), name="layer_norm_2")
    return out


_PARAM_NAMES = ("w_in", "ssd_conv_w", "ssd_conv_b", "ssd_dt_bias", "ssd_a_log", "ssd_d", "ssd_norm_w",
                "fox_f_bias", "gate_bias", "w_proj_ssd", "w_proj_att", "w_out", "ln1_g", "ln1_b",
                "w_up", "ffn_conv_w", "ffn_conv_b", "w_down", "ln2_g", "ln2_b")


def kernel(x, w_in, ssd_conv_w, ssd_conv_b, ssd_dt_bias, ssd_a_log, ssd_d, ssd_norm_w, fox_f_bias, gate_bias,
           w_proj_ssd, w_proj_att, w_out, ln1_g, ln1_b, w_up, ffn_conv_w, ffn_conv_b, w_down, ln2_g, ln2_b):
    params = (w_in, ssd_conv_w, ssd_conv_b, ssd_dt_bias, ssd_a_log, ssd_d, ssd_norm_w, fox_f_bias, gate_bias,
              w_proj_ssd, w_proj_att, w_out, ln1_g, ln1_b, w_up, ffn_conv_w, ffn_conv_b, w_down, ln2_g, ln2_b)
    batch, length, d = x.shape
    depth = w_in.shape[0]
    alpha = (2.0 * depth) ** 0.25
    h = x.reshape(batch * length, d)
    for layer in range(depth):
        p = {name: arr[layer] for name, arr in zip(_PARAM_NAMES, params)}
        h = _layer(h, p, batch=batch, length=length, alpha=alpha)
    return h.reshape(batch, length, d)
```

```python
import functools
import math

import jax
import jax.numpy as jnp
from jax import lax
from jax.experimental import pallas as pl
from jax.experimental.pallas import tpu as pltpu

F32 = jnp.float32
BF16 = jnp.bfloat16

SSD_HEAD_DIM = 64
SSD_STATE = 128
FOX_HEAD_DIM = 128
LN_EPS = 1e-5
RMS_EPS = 1e-5

LANES = 128
SUBLANES = 8
BF16_SUBLANES = 16
VMEM_LIMIT_BYTES = 56 * 1024 * 1024

SSD_CHUNK = 128
ATT_BLOCK = 512
ATT_HEADS_PER_STEP = 4


def _cparams(*sem):
    return pltpu.CompilerParams(dimension_semantics=sem, vmem_limit_bytes=VMEM_LIMIT_BYTES)


def _tile(n, pref, quantum=LANES):
    if n <= pref:
        return n
    t = (pref // quantum) * quantum
    while t > quantum and n % t:
        t -= quantum
    assert n % t == 0, (n, pref, quantum)
    return t


def _cast_slices(weights, steps, linear_step):
    specs, shapes = [], []
    for w in weights:
        rows, cols = w.shape
        assert rows % steps == 0 and (rows // steps) % BF16_SUBLANES == 0, (w.shape, steps)
        specs.append(pl.BlockSpec((rows // steps, cols), lambda *ids: (linear_step(*ids), 0)))
        shapes.append(jax.ShapeDtypeStruct((rows, cols), BF16))
    return specs, shapes


def _softplus(x):
    return jnp.maximum(x, 0.0) + jnp.log1p(jnp.exp(-jnp.abs(x)))


def _log_sigmoid(x):
    return jnp.minimum(x, 0.0) - jnp.log1p(jnp.exp(-jnp.abs(x)))


def _silu(x):
    h = 0.5 * x
    return h + h * jnp.tanh(h)


def _proj_body(a_ref, wt_ref, *rest, scaled_tiles, scale, sigmoid, n_cast):
    if sigmoid:
        b_ref, rest = rest[0], rest[1:]
    cast_in, o_ref, cast_out = rest[:n_cast], rest[n_cast], rest[n_cast + 1:]
    for src, dst in zip(cast_in, cast_out):
        dst[...] = src[...].astype(BF16)
    acc = lax.dot_general(a_ref[...], wt_ref[...], (((1,), (1,)), ((), ())), preferred_element_type=F32)
    if scaled_tiles:
        acc = acc * jnp.where(pl.program_id(1) < scaled_tiles, scale, 1.0)
    if sigmoid:
        acc = 0.5 + 0.5 * jnp.tanh(0.5 * (acc + b_ref[...]))
    o_ref[...] = acc.astype(o_ref.dtype)


def _cast_body(src_ref, dst_ref):
    dst_ref[...] = src_ref[...].astype(dst_ref.dtype)


def _cast_rows(w, first, rows, *, block_rows=640):
    assert first % SUBLANES == 0
    block_rows = _tile(rows, block_rows, BF16_SUBLANES)
    cols = w.shape[1]
    return pl.pallas_call(
        _cast_body,
        grid=(rows // block_rows,),
        in_specs=[pl.BlockSpec((pl.Element(block_rows), pl.Element(cols)),
                               lambda i: (pl.multiple_of(first + i * block_rows, SUBLANES), 0))],
        out_specs=pl.BlockSpec((block_rows, cols), lambda i: (i, 0)),
        out_shape=jax.ShapeDtypeStruct((rows, cols), BF16),
        compiler_params=_cparams("parallel"),
        name="weight_cast",
    )(w)


def _proj(a, wt, out_dtype, *, row0=0, n=None, bias=None, scaled_cols=0, scale=1.0, casts=(), tm=1024, tn=1024,
          lhs_buffers=2, name):
    m, k = a.shape
    n = wt.shape[0] if n is None else n
    tm, tn = _tile(m, tm), _tile(n, tn)
    assert scaled_cols % tn == 0 and row0 % BF16_SUBLANES == 0
    nj = n // tn
    steps = (m // tm) * nj
    in_specs = [pl.BlockSpec((tm, k), lambda i, j: (i, 0), pipeline_mode=pl.Buffered(lhs_buffers)),
                pl.BlockSpec((pl.Element(tn), pl.Element(k)),
                             lambda i, j: (pl.multiple_of(row0 + j * tn, BF16_SUBLANES), 0))]
    args = [a, wt]
    if bias is not None:
        in_specs.append(pl.BlockSpec((1, tn), lambda i, j: (0, j)))
        args.append(bias)
    out_specs = [pl.BlockSpec((tm, tn), lambda i, j: (i, j))]
    out_shape = [jax.ShapeDtypeStruct((m, n), out_dtype)]
    for w, first, rows in casts:
        assert first % SUBLANES == 0
        blk = next(b for b in range(BF16_SUBLANES, rows + 1, BF16_SUBLANES) if rows % b == 0 and rows // b <= steps)
        nblk = rows // blk
        slot = lambda i, j, nblk=nblk: jnp.minimum(i * nj + j, nblk - 1)
        in_specs.append(pl.BlockSpec((pl.Element(blk), pl.Element(w.shape[1])),
                                     lambda i, j, first=first, blk=blk, slot=slot:
                                     (pl.multiple_of(first + slot(i, j) * blk, SUBLANES), 0)))
        args.append(w)
        out_specs.append(pl.BlockSpec((blk, w.shape[1]), lambda i, j, slot=slot: (slot(i, j), 0)))
        out_shape.append(jax.ShapeDtypeStruct((rows, w.shape[1]), BF16))
    outs = pl.pallas_call(
        functools.partial(_proj_body, scaled_tiles=scaled_cols // tn, scale=scale, sigmoid=bias is not None,
                          n_cast=len(casts)),
        grid=(m // tm, nj),
        in_specs=in_specs,
        out_specs=out_specs,
        out_shape=out_shape,
        compiler_params=_cparams("parallel", "arbitrary"),
        name=name,
    )(*args)
    return outs[0] if not casts else outs


def _head_prep_body(dt_ref, f_ref, dtb_ref, alog_ref, fb_ref, acs_ref, dtt_ref, acst_ref, fcum_ref, carry_ref, *,
                    fox_heads):
    @pl.when(pl.program_id(1) == 0)
    def _():
        carry_ref[...] = jnp.zeros_like(carry_ref)

    q = dt_ref.shape[0]
    row = lax.broadcasted_iota(jnp.int32, (q, q), 0)
    col = lax.broadcasted_iota(jnp.int32, (q, q), 1)
    tri = (row >= col).astype(F32)
    dtv = _softplus(dt_ref[...] + dtb_ref[...])
    da = dtv * (-jnp.exp(alog_ref[...]))
    acs = jnp.dot(tri, da, preferred_element_type=F32, precision=lax.Precision.HIGHEST)
    acs_ref[...] = acs
    dtt_ref[...] = dtv.T
    acst_ref[...] = acs.T
    logf = _log_sigmoid(f_ref[...] + fb_ref[...])
    cs = jnp.dot(tri, logf, preferred_element_type=F32, precision=lax.Precision.HIGHEST) + carry_ref[...]
    carry_ref[...] = cs[q - 1:q, :]
    fcum_ref[...] = cs.T[0:fox_heads, :]


def _head_prep(small, dt_bias, a_log, f_bias, *, batch, length, fox_heads):
    m = small.shape[0]
    q = SSD_CHUNK
    nc = length // q
    nat = lambda lane_blk: pl.BlockSpec((q, LANES), lambda b, c: (b * nc + c, lane_blk))
    tr = pl.BlockSpec((LANES, q), lambda b, c: (0, b * nc + c))
    vec = pl.BlockSpec((1, LANES), lambda b, c: (0, 0))
    return pl.pallas_call(
        functools.partial(_head_prep_body, fox_heads=fox_heads),
        grid=(batch, nc),
        in_specs=[nat(0), nat(1), vec, vec, vec],
        out_specs=[nat(0), tr, tr, pl.BlockSpec((None, fox_heads, q), lambda b, c: (b, 0, c))],
        out_shape=[jax.ShapeDtypeStruct((m, LANES), F32), jax.ShapeDtypeStruct((LANES, m), F32),
                   jax.ShapeDtypeStruct((LANES, m), F32), jax.ShapeDtypeStruct((batch, fox_heads, length), F32)],
        scratch_shapes=[pltpu.VMEM((1, LANES), F32)],
        compiler_params=_cparams("parallel", "arbitrary"),
        name="head_prep",
    )(small, small, dt_bias, a_log, f_bias)


def _ssd_body(z_ref, xs_ref, b_ref, c_ref, acs_ref, dtt_ref, acst_ref, px_ref, pb_ref, pc_ref, y_ref,
              h_ref, tail_ref, ybuf_ref, *, q, r, kw):
    g = pl.program_id(1)
    c = pl.program_id(2)
    gw = r * SSD_HEAD_DIM
    n = SSD_STATE

    @pl.when(c == 0)
    def _():
        h_ref[...] = jnp.zeros_like(h_ref)
        tail_ref[...] = jnp.zeros_like(tail_ref)

    def conv_silu(cur_ref, lo, hi, p_ref):
        cur = cur_ref[...].astype(F32)
        p = p_ref[...]
        bias = p[kw:kw + 1, :]
        head_src = jnp.concatenate([tail_ref[:, lo:hi], cur[0:SUBLANES, :]], axis=0)
        head = bias + p[kw - 1:kw, :] * head_src[SUBLANES:2 * SUBLANES, :]
        out = p[0:1, :] * cur
        for k in range(kw - 1):
            shift = kw - 1 - k
            out = p[k + 1:k + 2, :] * cur + pltpu.roll(out, 1, axis=0)
            head = head + p[k:k + 1, :] * head_src[SUBLANES - shift:2 * SUBLANES - shift, :]
        out = bias + out
        tail_ref[:, lo:hi] = cur[q - SUBLANES:q, :]
        return _silu(jnp.concatenate([head, out[SUBLANES:, :]], axis=0))

    xs = conv_silu(xs_ref, 0, gw, px_ref)
    bm = conv_silu(b_ref, gw, gw + n, pb_ref)
    cm = conv_silu(c_ref, gw + n, gw + 2 * n, pc_ref)

    row = lax.broadcasted_iota(jnp.int32, (q, q), 0)
    col = lax.broadcasted_iota(jnp.int32, (q, q), 1)
    tri = row >= col
    acs_g = pltpu.roll(acs_ref[...], (LANES - g * r) % LANES, axis=1)
    dt_t = dtt_ref[...]
    acs_t = acst_ref[...]

    cm_bf = cm.astype(BF16)
    cb = lax.dot_general(cm_bf, bm.astype(BF16), (((1,), (1,)), ((), ())), preferred_element_type=F32)
    bm_t = bm.T

    lane = lax.broadcasted_iota(jnp.int32, (q, LANES), 1)
    lo_half = lane < SSD_HEAD_DIM
    lo_half_row = lo_half[0:1, :]
    pairs = range(r // 2)
    pair_cols = [slice(j * LANES, (j + 1) * LANES) for j in pairs]
    y_off = jnp.dot(cm_bf, h_ref[...].astype(BF16), preferred_element_type=F32)

    lhs_y, lhs_s, rhs, e_pair, cd_pair = [], [], [], [], []
    for j in pairs:
        xs_p = xs[:, pair_cols[j]]
        rhs.append(jnp.concatenate([jnp.where(lo_half, xs_p, 0.0).astype(BF16),
                                    jnp.where(lo_half, 0.0, xs_p).astype(BF16)], axis=0))
        m_parts, bw_parts, e_cols, cd = [], [], [], []
        for hd in (2 * j, 2 * j + 1):
            a_col = acs_g[:, hd:hd + 1]
            a_row = acs_t[hd:hd + 1, :]
            dt_row = dt_t[hd:hd + 1, :]
            a_last = acs_g[q - 1:q, hd:hd + 1]
            lmat = jnp.exp(jnp.where(tri, a_col - a_row, -jnp.inf))
            m_parts.append(cb * lmat * dt_row)
            bw_parts.append(bm_t * (jnp.exp(a_last - a_row) * dt_row))
            e_cols.append(jnp.exp(a_col))
            cd.append(jnp.exp(a_last))
        lhs_y.append(jnp.concatenate(m_parts, axis=1).astype(BF16))
        lhs_s.append(jnp.concatenate(bw_parts, axis=1).astype(BF16))
        e_pair.append(jnp.where(lo_half, e_cols[0], e_cols[1]))
        cd_pair.append(jnp.where(lo_half_row, cd[0], cd[1]))
    y_diag = [jnp.dot(lhs_y[j], rhs[j], preferred_element_type=F32) for j in pairs]
    s_new = [jnp.dot(lhs_s[j], rhs[j], preferred_element_type=F32) for j in pairs]
    ssq = jnp.zeros((q, 1), F32)
    for j in pairs:
        cols = pair_cols[j]
        h_ref[:, cols] = cd_pair[j] * h_ref[:, cols] + s_new[j]
        y = y_diag[j] + e_pair[j] * y_off[:, cols] + px_ref[kw + 1:kw + 2, cols] * xs[:, cols]
        y = y * _silu(z_ref[:, cols].astype(F32))
        ybuf_ref[:, cols] = y
        ssq = ssq + jnp.sum(y * y, axis=1, keepdims=True)
    inv = lax.rsqrt(ssq / gw + RMS_EPS)
    y_ref[...] = (ybuf_ref[...] * inv * px_ref[kw + 2:kw + 3, :]).astype(y_ref.dtype)


def _ssd(zx, acs, dt_t, acs_t, conv_w, conv_b, d_cols, norm_w, *, batch, length, d_inner, groups):
    m = zx.shape[0]
    q = SSD_CHUNK
    assert length % q == 0
    nc = length // q
    gw = d_inner // groups
    r = gw // SSD_HEAD_DIM
    assert r % 2 == 0 and r % SUBLANES == 0 and r * groups <= LANES
    kw = conv_w.shape[0]
    n = SSD_STATE
    conv_dim = conv_w.shape[1]
    rows = jnp.concatenate([conv_w, conv_b], axis=0)
    extra = jnp.zeros((2, conv_dim), F32).at[0, :d_inner].set(d_cols[0]).at[1, :d_inner].set(norm_w[0])
    params = jnp.concatenate([rows, extra], axis=0)
    np_rows = params.shape[0]
    zblk = d_inner // gw
    bblk = 2 * d_inner // n
    cblk_w = d_inner // n
    row = lambda b, g, c: b * nc + c
    in_specs = [
        pl.BlockSpec((q, gw), lambda b, g, c: (row(b, g, c), g)),
        pl.BlockSpec((q, gw), lambda b, g, c: (row(b, g, c), zblk + g)),
        pl.BlockSpec((q, n), lambda b, g, c: (row(b, g, c), bblk + g)),
        pl.BlockSpec((q, n), lambda b, g, c: (row(b, g, c), bblk + groups + g)),
        pl.BlockSpec((q, LANES), lambda b, g, c: (row(b, g, c), 0)),
        pl.BlockSpec((r, q), lambda b, g, c: (g, row(b, g, c))),
        pl.BlockSpec((r, q), lambda b, g, c: (g, row(b, g, c))),
        pl.BlockSpec((np_rows, gw), lambda b, g, c: (0, g)),
        pl.BlockSpec((np_rows, n), lambda b, g, c: (0, cblk_w + g)),
        pl.BlockSpec((np_rows, n), lambda b, g, c: (0, cblk_w + groups + g)),
    ]
    return pl.pallas_call(
        functools.partial(_ssd_body, q=q, r=r, kw=kw),
        grid=(batch, groups, nc),
        in_specs=in_specs,
        out_specs=pl.BlockSpec((q, gw), lambda b, g, c: (row(b, g, c), g)),
        out_shape=jax.ShapeDtypeStruct((m, d_inner), BF16),
        scratch_shapes=[
            pltpu.VMEM((n, gw), F32),
            pltpu.VMEM((SUBLANES, gw + 2 * n), F32),
            pltpu.VMEM((q, gw), F32),
        ],
        compiler_params=_cparams("parallel", "parallel", "arbitrary"),
        name="ssd_scan",
    )(zx, zx, zx, zx, acs, dt_t, acs_t, params, params, params)


def _attn_body(q_ref, k_ref, v_ref, f_ref, *rest, blk, hp, n_cast):
    cast_in, o_ref, cast_out = rest[:n_cast], rest[n_cast], rest[n_cast + 1:2 * n_cast + 1]
    vaug_ref, m_ref, acc_ref = rest[2 * n_cast + 1:]
    for src, dst in zip(cast_in, cast_out):
        dst[...] = src[...].astype(BF16)
    qi = pl.program_id(2)
    dh = FOX_HEAD_DIM
    length = k_ref.shape[0]

    @pl.when(qi == 0)
    def _():
        for h in range(hp):
            vaug_ref[h, :, 0:dh] = v_ref[:, h * dh:(h + 1) * dh]
            vaug_ref[h, :, dh:2 * dh] = jnp.ones((length, dh), BF16)

    m_ref[...] = jnp.full_like(m_ref, -jnp.inf)
    acc_ref[...] = jnp.zeros_like(acc_ref)

    def step(j, masked):
        start = pl.multiple_of(j * blk, blk)

        def scores(h):
            qv = q_ref[:, h * dh:(h + 1) * dh]
            kj = k_ref[pl.ds(start, blk), h * dh:(h + 1) * dh]
            return lax.dot_general(qv, kj, (((1,), (1,)), ((), ())), preferred_element_type=F32)

        if masked:
            causal = (lax.broadcasted_iota(jnp.int32, (blk, blk), 0)
                      >= lax.broadcasted_iota(jnp.int32, (blk, blk), 1))

        def softmax_part(h, s):
            s = s - f_ref[h, pl.ds(j, 1), :]
            if masked:
                s = jnp.where(causal, s, -jnp.inf)
            m_old = m_ref[h]
            m_new = jnp.maximum(m_old, jnp.max(s, axis=1, keepdims=True))
            m_ref[h] = m_new
            return jnp.exp(s - jnp.tile(m_new, (1, blk // LANES))).astype(BF16), jnp.exp(m_old - m_new)

        def accumulate(h, p, alpha):
            pv = jnp.dot(p, vaug_ref[h, pl.ds(start, blk), :], preferred_element_type=F32)
            acc_ref[h] = jnp.tile(alpha, (1, 2)) * acc_ref[h] + pv

        s_cur = scores(0)
        pending = None
        for h in range(hp):
            s_nxt = scores(h + 1) if h + 1 < hp else None
            p, alpha = softmax_part(h, s_cur)
            if pending is not None:
                accumulate(*pending)
            pending = (h, p, alpha)
            s_cur = s_nxt
        accumulate(*pending)

    def loop_body(j, carry):
        step(j, False)
        return carry

    lax.fori_loop(0, qi, loop_body, 0)
    step(qi, True)
    for h in range(hp):
        acc = acc_ref[h]
        o_ref[:, h * dh:(h + 1) * dh] = (acc[:, 0:dh] / acc[:, dh:2 * dh]).astype(o_ref.dtype)


def _attention(qkv, fcum_t, cast_weights, *, batch, length, heads):
    m = qkv.shape[0]
    dh = FOX_HEAD_DIM
    hp = ATT_HEADS_PER_STEP
    assert dh == LANES and heads % hp == 0
    blk = _tile(length, ATT_BLOCK)
    nq = length // blk
    ng = heads // hp
    fcum_t = fcum_t.reshape(batch * ng, hp, nq, blk)
    cast_specs, cast_shapes = _cast_slices(cast_weights, batch * ng * nq, lambda b, g, i: (b * ng + g) * nq + i)
    outs = pl.pallas_call(
        functools.partial(_attn_body, blk=blk, hp=hp, n_cast=len(cast_weights)),
        grid=(batch, ng, nq),
        in_specs=[
            pl.BlockSpec((blk, hp * dh), lambda b, g, i: (b * nq + i, g)),
            pl.BlockSpec((length, hp * dh), lambda b, g, i: (b, ng + g)),
            pl.BlockSpec((length, hp * dh), lambda b, g, i: (b, 2 * ng + g)),
            pl.BlockSpec((None, hp, nq, blk), lambda b, g, i: (b * ng + g, 0, 0, 0)),
        ] + cast_specs,
        out_specs=[pl.BlockSpec((blk, hp * dh), lambda b, g, i: (b * nq + i, g))] + cast_specs,
        out_shape=[jax.ShapeDtypeStruct((m, heads * dh), BF16)] + cast_shapes,
        scratch_shapes=[pltpu.VMEM((hp, length, 2 * dh), BF16),
                        pltpu.VMEM((hp, blk, LANES), F32),
                        pltpu.VMEM((hp, blk, 2 * dh), F32)],
        compiler_params=_cparams("parallel", "parallel", "arbitrary"),
        name="fox_attention",
    )(qkv, qkv, qkv, fcum_t, *cast_weights)
    return outs[0], outs[1:]


def _merge_body(ys_ref, ya_ref, ws_ref, wa_ref, gs_ref, ga_ref, cast_in, o_ref, cast_out):
    cast_out[...] = cast_in[...].astype(BF16)
    ps = jnp.dot(ys_ref[...], ws_ref[...], preferred_element_type=F32)
    pa = jnp.dot(ya_ref[...], wa_ref[...], preferred_element_type=F32)
    o_ref[...] = (gs_ref[...].astype(F32) * ps + ga_ref[...].astype(F32) * pa).astype(o_ref.dtype)


def _merge(y_ssd, y_att, w_ssd, w_att, gates, cast_weight, *, tm=1024, tn=256):
    m, ks = y_ssd.shape
    ka = y_att.shape[1]
    n = w_ssd.shape[1]
    tm, tn = _tile(m, tm), _tile(n, tn)
    nj = n // tn
    once = pl.Buffered(1)
    cast_specs, cast_shapes = _cast_slices([cast_weight], (m // tm) * nj, lambda i, j: i * nj + j)
    return pl.pallas_call(
        _merge_body,
        grid=(m // tm, nj),
        in_specs=[
            pl.BlockSpec((tm, ks), lambda i, j: (i, 0), pipeline_mode=once),
            pl.BlockSpec((tm, ka), lambda i, j: (i, 0), pipeline_mode=once),
            pl.BlockSpec((ks, tn), lambda i, j: (0, j)),
            pl.BlockSpec((ka, tn), lambda i, j: (0, j)),
            pl.BlockSpec((tm, tn), lambda i, j: (i, j)),
            pl.BlockSpec((tm, tn), lambda i, j: (i, nj + j)),
        ] + cast_specs,
        out_specs=[pl.BlockSpec((tm, tn), lambda i, j: (i, j))] + cast_specs,
        out_shape=[jax.ShapeDtypeStruct((m, n), BF16)] + cast_shapes,
        compiler_params=_cparams("parallel", "arbitrary"),
        name="merge_proj",
    )(y_ssd, y_att, w_ssd, w_att, gates, gates, cast_weight)


def _mm_residual_body(a_ref, w_ref, r_ref, o_ref, *, alpha):
    acc = jnp.dot(a_ref[...], w_ref[...], preferred_element_type=F32)
    o_ref[...] = alpha * r_ref[...] + acc


def _matmul_residual(a, w, resid, alpha, *, tm, tn, lhs_buffers=2, name):
    m, k = a.shape
    n = w.shape[1]
    tm, tn = _tile(m, tm), _tile(n, tn)
    return pl.pallas_call(
        functools.partial(_mm_residual_body, alpha=alpha),
        grid=(m // tm, n // tn),
        in_specs=[
            pl.BlockSpec((tm, k), lambda i, j: (i, 0), pipeline_mode=pl.Buffered(lhs_buffers)),
            pl.BlockSpec((k, tn), lambda i, j: (0, j)),
            pl.BlockSpec((tm, tn), lambda i, j: (i, j)),
        ],
        out_specs=pl.BlockSpec((tm, tn), lambda i, j: (i, j)),
        out_shape=jax.ShapeDtypeStruct((m, n), F32),
        compiler_params=_cparams("parallel", "arbitrary"),
        name=name,
    )(a, w, resid)


def _ln_body(x_ref, g_ref, b_ref, *o_refs):
    x = x_ref[...]
    mu = jnp.mean(x, axis=-1, keepdims=True)
    xc = x - mu
    var = jnp.mean(xc * xc, axis=-1, keepdims=True)
    out = xc * lax.rsqrt(var + LN_EPS) * g_ref[...] + b_ref[...]
    for o_ref in o_refs:
        o_ref[...] = out.astype(o_ref.dtype)


def _layer_norm(x, gain, bias, out_dtypes, *, tm=256, name):
    m, d = x.shape
    tm = _tile(m, tm, SUBLANES)
    outs = pl.pallas_call(
        _ln_body,
        grid=(m // tm,),
        in_specs=[pl.BlockSpec((tm, d), lambda i: (i, 0)),
                  pl.BlockSpec((1, d), lambda i: (0, 0)),
                  pl.BlockSpec((1, d), lambda i: (0, 0))],
        out_specs=[pl.BlockSpec((tm, d), lambda i: (i, 0)) for _ in out_dtypes],
        out_shape=[jax.ShapeDtypeStruct((m, d), dt) for dt in out_dtypes],
        compiler_params=_cparams("parallel"),
        name=name,
    )(x, gain, bias)
    return outs


def _ffn_up_body(a_ref, halo_ref, wv_ref, wg_ref, cwv_ref, cwg_ref, cbv_ref, cbg_ref, cast_in, o_ref, cast_out,
                 wbf_ref, *, kw, tiles_per_seq):
    i = pl.program_id(1)
    pad = BF16_SUBLANES
    cast_out[...] = cast_in[...].astype(BF16)

    @pl.when(i == 0)
    def _():
        wbf_ref[0] = wv_ref[...].astype(BF16)
        wbf_ref[1] = wg_ref[...].astype(BF16)

    a = a_ref[...]
    halo = halo_ref[...]
    keep = (i % tiles_per_seq != 0).astype(F32)

    def conv(c, cw_ref, cb_ref):
        w = wbf_ref[c]
        u = jnp.dot(a, w, preferred_element_type=F32)
        uh = jnp.dot(halo, w, preferred_element_type=F32) * keep
        cw = cw_ref[...]
        cb = cb_ref[...]
        head_src = jnp.concatenate([uh, u[0:pad, :]], axis=0)
        head = cb + cw[kw - 1:kw, :] * head_src[pad:2 * pad, :]
        out = cw[0:1, :] * u
        for k in range(kw - 1):
            shift = kw - 1 - k
            out = cw[k + 1:k + 2, :] * u + pltpu.roll(out, 1, axis=0)
            head = head + cw[k:k + 1, :] * head_src[pad - shift:2 * pad - shift, :]
        return cb + out, head

    gate, gate_head = conv(1, cwg_ref, cbg_ref)
    gate, gate_head = _silu(gate), _silu(gate_head)
    val, val_head = conv(0, cwv_ref, cbv_ref)
    o_ref[...] = (gate * val).astype(o_ref.dtype)
    o_ref[0:pad, :] = (gate_head * val_head).astype(o_ref.dtype)


def _ffn_up(h, w_up, conv_w, conv_b, cast_weight, *, length, d_ff, tm=1024, tn=256):
    m, k = h.shape
    tm = _tile(min(m, length), tm, BF16_SUBLANES)
    assert length % tm == 0
    tn = _tile(d_ff, tn)
    nj = d_ff // tn
    ni = m // tm
    kw = conv_w.shape[0]
    hb = tm // BF16_SUBLANES
    cast_specs, cast_shapes = _cast_slices([cast_weight], nj * ni, lambda j, i: j * ni + i)
    return pl.pallas_call(
        functools.partial(_ffn_up_body, kw=kw, tiles_per_seq=length // tm),
        grid=(nj, m // tm),
        in_specs=[
            pl.BlockSpec((tm, k), lambda j, i: (i, 0)),
            pl.BlockSpec((BF16_SUBLANES, k), lambda j, i: (jnp.maximum(i * hb - 1, 0), 0)),
            pl.BlockSpec((k, tn), lambda j, i: (0, j)),
            pl.BlockSpec((k, tn), lambda j, i: (0, nj + j)),
            pl.BlockSpec((kw, tn), lambda j, i: (0, j)),
            pl.BlockSpec((kw, tn), lambda j, i: (0, nj + j)),
            pl.BlockSpec((1, tn), lambda j, i: (0, j)),
            pl.BlockSpec((1, tn), lambda j, i: (0, nj + j)),
        ] + cast_specs,
        out_specs=[pl.BlockSpec((tm, tn), lambda j, i: (i, j))] + cast_specs,
        out_shape=[jax.ShapeDtypeStruct((m, d_ff), BF16)] + cast_shapes,
        scratch_shapes=[pltpu.VMEM((2, k, tn), BF16)],
        compiler_params=_cparams("parallel", "arbitrary"),
        name="ffn_up_conv_act",
    )(h, h, w_up, w_up, conv_w, conv_w, conv_b, conv_b, cast_weight)


def _layer(h, p, *, batch, length, alpha):
    m, d = h.shape
    d_inner = p["ssd_norm_w"].shape[-1]
    conv_dim = p["ssd_conv_b"].shape[-1]
    ssd_heads = p["ssd_dt_bias"].shape[-1]
    fox_heads = p["fox_f_bias"].shape[-1]
    d_att = fox_heads * FOX_HEAD_DIM
    d_ff = p["w_down"].shape[0]
    groups = (conv_dim - d_inner) // (2 * SSD_STATE)
    assert ssd_heads <= LANES and fox_heads <= LANES

    o_z, o_xbc = 0, d_inner
    o_dt = o_xbc + conv_dim
    o_q = o_dt + ssd_heads
    o_f = o_q + 3 * d_att
    o_g = o_f + fox_heads
    wt_f32 = p["w_in"].T
    wt_head = _cast_rows(wt_f32, 0, o_q)
    zeros = lambda n: jnp.zeros((n, d), BF16)
    wt_small = jnp.concatenate([wt_head[o_dt:o_q], zeros(LANES - ssd_heads),
                                _cast_rows(wt_f32, o_f, fox_heads), zeros(LANES - fox_heads)], axis=0)

    h_bf = h.astype(BF16)
    zx, wt_qkv, wt_gate = _proj(h_bf, wt_head, F32, row0=o_z, n=o_dt,
                                casts=[(wt_f32, o_q, 3 * d_att), (wt_f32, o_g, 2 * d)], lhs_buffers=1,
                                name="in_proj_zx")
    qkv = _proj(h_bf, wt_qkv, BF16, scaled_cols=d_att, scale=1.0 / math.sqrt(FOX_HEAD_DIM), name="in_proj_qkv")
    gates = _proj(h_bf, wt_gate, F32, bias=p["gate_bias"].reshape(1, 2 * d), name="in_proj_gates")
    small = _proj(h_bf, wt_small, F32, name="in_proj_small")

    pad_row = lambda v, n: jnp.pad(v.reshape(1, -1).astype(F32), ((0, 0), (0, n - v.shape[-1])))
    acs, dt_t, acs_t, fcum = _head_prep(small, pad_row(p["ssd_dt_bias"], LANES), pad_row(p["ssd_a_log"], LANES),
                                        pad_row(p["fox_f_bias"], LANES),
                                        batch=batch, length=length, fox_heads=fox_heads)
    y_ssd = _ssd(zx, acs, dt_t, acs_t, p["ssd_conv_w"], p["ssd_conv_b"].reshape(1, -1),
                 jnp.repeat(p["ssd_d"].astype(F32), SSD_HEAD_DIM).reshape(1, -1),
                 p["ssd_norm_w"].reshape(1, -1),
                 batch=batch, length=length, d_inner=d_inner, groups=groups)
    y_att, (w_ssd_bf, w_att_bf) = _attention(qkv, fcum, [p["w_proj_ssd"], p["w_proj_att"]],
                                             batch=batch, length=length, heads=fox_heads)

    merged, w_out_bf = _merge(y_ssd, y_att, w_ssd_bf, w_att_bf, gates, p["w_out"])
    pre1 = _matmul_residual(merged, w_out_bf, h, alpha, tm=1024, tn=512, name="out_proj_residual")
    h1, h1_bf = _layer_norm(pre1, p["ln1_g"].reshape(1, -1), p["ln1_b"].reshape(1, -1), (F32, BF16), name="layer_norm_1")

    act, w_down_bf = _ffn_up(h1_bf, p["w_up"], p["ffn_conv_w"], p["ffn_conv_b"].reshape(1, -1), p["w_down"],
                             length=length, d_ff=d_ff)
    pre2 = _matmul_residual(act, w_down_bf, h1, alpha, tm=1024, tn=256, lhs_buffers=1,
                            name="ffn_down_residual")
    (out,) = _layer_norm(pre2, p["ln2_g"].reshape(1, -1), p["ln2_b"].reshape(1, -1), (F32,), name="layer_norm_2")
    return out


_PARAM_NAMES = ("w_in", "ssd_conv_w", "ssd_conv_b", "ssd_dt_bias", "ssd_a_log", "ssd_d", "ssd_norm_w",
                "fox_f_bias", "gate_bias", "w_proj_ssd", "w_proj_att", "w_out", "ln1_g", "ln1_b",
                "w_up", "ffn_conv_w", "ffn_conv_b", "w_down", "ln2_g", "ln2_b")


def kernel(x, w_in, ssd_conv_w, ssd_conv_b, ssd_dt_bias, ssd_a_log, ssd_d, ssd_norm_w, fox_f_bias, gate_bias,
           w_proj_ssd, w_proj_att, w_out, ln1_g, ln1_b, w_up, ffn_conv_w, ffn_conv_b, w_down, ln2_g, ln2_b):
    params = (w_in, ssd_conv_w, ssd_conv_b, ssd_dt_bias, ssd_a_log, ssd_d, ssd_norm_w, fox_f_bias, gate_bias,
              w_proj_ssd, w_proj_att, w_out, ln1_g, ln1_b, w_up, ffn_conv_w, ffn_conv_b, w_down, ln2_g, ln2_b)
    batch, length, d = x.shape
    depth = w_in.shape[0]
    alpha = (2.0 * depth) ** 0.25
    h = x.reshape(batch * length, d)
    for layer in range(depth):
        p = {name: arr[layer] for name, arr in zip(_PARAM_NAMES, params)}
        h = _layer(h, p, batch=batch, length=length, alpha=alpha)
    return h.reshape(batch, length, d)
```

```python
import functools
import math

import jax
import jax.numpy as jnp
from jax import lax
from jax.experimental import pallas as pl
from jax.experimental.pallas import tpu as pltpu

F32 = jnp.float32
BF16 = jnp.bfloat16

SSD_HEAD_DIM = 64
SSD_STATE = 128
FOX_HEAD_DIM = 128
LN_EPS = 1e-5
RMS_EPS = 1e-5

LANES = 128
SUBLANES = 8
BF16_SUBLANES = 16
VMEM_LIMIT_BYTES = 56 * 1024 * 1024

SSD_CHUNK = 128
ATT_BLOCK = 512
ATT_HEADS_PER_STEP = 4


def _cparams(*sem):
    return pltpu.CompilerParams(dimension_semantics=sem, vmem_limit_bytes=VMEM_LIMIT_BYTES)


def _tile(n, pref, quantum=LANES):
    if n <= pref:
        return n
    t = (pref // quantum) * quantum
    while t > quantum and n % t:
        t -= quantum
    assert n % t == 0, (n, pref, quantum)
    return t


def _cast_slices(weights, steps, linear_step):
    specs, shapes = [], []
    for w in weights:
        rows, cols = w.shape
        assert rows % steps == 0 and (rows // steps) % BF16_SUBLANES == 0, (w.shape, steps)
        specs.append(pl.BlockSpec((rows // steps, cols), lambda *ids: (linear_step(*ids), 0)))
        shapes.append(jax.ShapeDtypeStruct((rows, cols), BF16))
    return specs, shapes


def _softplus(x):
    return jnp.maximum(x, 0.0) + jnp.log1p(jnp.exp(-jnp.abs(x)))


def _log_sigmoid(x):
    return jnp.minimum(x, 0.0) - jnp.log1p(jnp.exp(-jnp.abs(x)))


def _silu(x):
    h = 0.5 * x
    return h + h * jnp.tanh(h)


def _proj_body(a_ref, wt_ref, *rest, scaled_tiles, scale, sigmoid, n_cast):
    if sigmoid:
        b_ref, rest = rest[0], rest[1:]
    cast_in, o_ref, cast_out = rest[:n_cast], rest[n_cast], rest[n_cast + 1:]
    for src, dst in zip(cast_in, cast_out):
        dst[...] = src[...].astype(BF16)
    acc = lax.dot_general(a_ref[...], wt_ref[...], (((1,), (1,)), ((), ())), preferred_element_type=F32)
    if scaled_tiles:
        acc = acc * jnp.where(pl.program_id(1) < scaled_tiles, scale, 1.0)
    if sigmoid:
        acc = 0.5 + 0.5 * jnp.tanh(0.5 * (acc + b_ref[...]))
    o_ref[...] = acc.astype(o_ref.dtype)


def _cast_body(src_ref, dst_ref):
    dst_ref[...] = src_ref[...].astype(dst_ref.dtype)


def _cast_rows(w, first, rows, *, block_rows=640):
    assert first % SUBLANES == 0
    block_rows = _tile(rows, block_rows, BF16_SUBLANES)
    cols = w.shape[1]
    return pl.pallas_call(
        _cast_body,
        grid=(rows // block_rows,),
        in_specs=[pl.BlockSpec((pl.Element(block_rows), pl.Element(cols)),
                               lambda i: (pl.multiple_of(first + i * block_rows, SUBLANES), 0))],
        out_specs=pl.BlockSpec((block_rows, cols), lambda i: (i, 0)),
        out_shape=jax.ShapeDtypeStruct((rows, cols), BF16),
        compiler_params=_cparams("parallel"),
        name="weight_cast",
    )(w)


def _proj(a, wt, out_dtype, *, row0=0, n=None, bias=None, scaled_cols=0, scale=1.0, casts=(), tm=1024, tn=1024,
          lhs_buffers=2, name):
    m, k = a.shape
    n = wt.shape[0] if n is None else n
    tm, tn = _tile(m, tm), _tile(n, tn)
    assert scaled_cols % tn == 0 and row0 % BF16_SUBLANES == 0
    nj = n // tn
    steps = (m // tm) * nj
    in_specs = [pl.BlockSpec((tm, k), lambda i, j: (i, 0), pipeline_mode=pl.Buffered(lhs_buffers)),
                pl.BlockSpec((pl.Element(tn), pl.Element(k)),
                             lambda i, j: (pl.multiple_of(row0 + j * tn, BF16_SUBLANES), 0))]
    args = [a, wt]
    if bias is not None:
        in_specs.append(pl.BlockSpec((1, tn), lambda i, j: (0, j)))
        args.append(bias)
    out_specs = [pl.BlockSpec((tm, tn), lambda i, j: (i, j))]
    out_shape = [jax.ShapeDtypeStruct((m, n), out_dtype)]
    for w, first, rows in casts:
        assert first % SUBLANES == 0
        blk = next(b for b in range(BF16_SUBLANES, rows + 1, BF16_SUBLANES) if rows % b == 0 and rows // b <= steps)
        nblk = rows // blk
        slot = lambda i, j, nblk=nblk: jnp.minimum(i * nj + j, nblk - 1)
        in_specs.append(pl.BlockSpec((pl.Element(blk), pl.Element(w.shape[1])),
                                     lambda i, j, first=first, blk=blk, slot=slot:
                                     (pl.multiple_of(first + slot(i, j) * blk, SUBLANES), 0)))
        args.append(w)
        out_specs.append(pl.BlockSpec((blk, w.shape[1]), lambda i, j, slot=slot: (slot(i, j), 0)))
        out_shape.append(jax.ShapeDtypeStruct((rows, w.shape[1]), BF16))
    outs = pl.pallas_call(
        functools.partial(_proj_body, scaled_tiles=scaled_cols // tn, scale=scale, sigmoid=bias is not None,
                          n_cast=len(casts)),
        grid=(m // tm, nj),
        in_specs=in_specs,
        out_specs=out_specs,
        out_shape=out_shape,
        compiler_params=_cparams("parallel", "arbitrary"),
        name=name,
    )(*args)
    return outs[0] if not casts else outs


def _head_prep_body(h_ref, wt_ref, dtb_ref, alog_ref, fb_ref, acs_ref, dtt_ref, acst_ref, fcum_ref, carry_ref, *,
                    fox_heads):
    @pl.when(pl.program_id(1) == 0)
    def _():
        carry_ref[...] = jnp.zeros_like(carry_ref)

    q = h_ref.shape[0]
    small = lax.dot_general(h_ref[...], wt_ref[...], (((1,), (1,)), ((), ())), preferred_element_type=F32)
    dt_raw, f_raw = small[:, 0:LANES], small[:, LANES:2 * LANES]
    row = lax.broadcasted_iota(jnp.int32, (q, q), 0)
    col = lax.broadcasted_iota(jnp.int32, (q, q), 1)
    tri = (row >= col).astype(F32)
    dtv = _softplus(dt_raw + dtb_ref[...])
    da = dtv * (-jnp.exp(alog_ref[...]))
    acs = jnp.dot(tri, da, preferred_element_type=F32, precision=lax.Precision.HIGHEST)
    acs_ref[...] = acs
    dtt_ref[...] = dtv.T
    acst_ref[...] = acs.T
    logf = _log_sigmoid(f_raw + fb_ref[...])
    cs = jnp.dot(tri, logf, preferred_element_type=F32, precision=lax.Precision.HIGHEST) + carry_ref[...]
    carry_ref[...] = cs[q - 1:q, :]
    fcum_ref[...] = cs.T[0:fox_heads, :]


def _head_prep(h, wt_small, dt_bias, a_log, f_bias, *, batch, length, fox_heads):
    m, k = h.shape
    q = SSD_CHUNK
    nc = length // q
    nat = pl.BlockSpec((q, LANES), lambda b, c: (b * nc + c, 0))
    tr = pl.BlockSpec((LANES, q), lambda b, c: (0, b * nc + c))
    vec = pl.BlockSpec((1, LANES), lambda b, c: (0, 0))
    return pl.pallas_call(
        functools.partial(_head_prep_body, fox_heads=fox_heads),
        grid=(batch, nc),
        in_specs=[pl.BlockSpec((q, k), lambda b, c: (b * nc + c, 0)),
                  pl.BlockSpec((2 * LANES, k), lambda b, c: (0, 0)), vec, vec, vec],
        out_specs=[nat, tr, tr, pl.BlockSpec((None, fox_heads, q), lambda b, c: (b, 0, c))],
        out_shape=[jax.ShapeDtypeStruct((m, LANES), F32), jax.ShapeDtypeStruct((LANES, m), F32),
                   jax.ShapeDtypeStruct((LANES, m), F32), jax.ShapeDtypeStruct((batch, fox_heads, length), F32)],
        scratch_shapes=[pltpu.VMEM((1, LANES), F32)],
        compiler_params=_cparams("parallel", "arbitrary"),
        name="head_prep",
    )(h, wt_small, dt_bias, a_log, f_bias)


def _ssd_body(z_ref, xs_ref, b_ref, c_ref, acs_ref, dtt_ref, acst_ref, px_ref, pb_ref, pc_ref, y_ref,
              h_ref, tail_ref, ybuf_ref, *, q, r, kw):
    g = pl.program_id(1)
    c = pl.program_id(2)
    gw = r * SSD_HEAD_DIM
    n = SSD_STATE

    @pl.when(c == 0)
    def _():
        h_ref[...] = jnp.zeros_like(h_ref)
        tail_ref[...] = jnp.zeros_like(tail_ref)

    def conv_silu(cur_ref, lo, hi, p_ref):
        cur = cur_ref[...].astype(F32)
        p = p_ref[...]
        bias = p[kw:kw + 1, :]
        head_src = jnp.concatenate([tail_ref[:, lo:hi], cur[0:SUBLANES, :]], axis=0)
        head = bias + p[kw - 1:kw, :] * head_src[SUBLANES:2 * SUBLANES, :]
        out = p[0:1, :] * cur
        for k in range(kw - 1):
            shift = kw - 1 - k
            out = p[k + 1:k + 2, :] * cur + pltpu.roll(out, 1, axis=0)
            head = head + p[k:k + 1, :] * head_src[SUBLANES - shift:2 * SUBLANES - shift, :]
        out = bias + out
        tail_ref[:, lo:hi] = cur[q - SUBLANES:q, :]
        return _silu(jnp.concatenate([head, out[SUBLANES:, :]], axis=0))

    xs = conv_silu(xs_ref, 0, gw, px_ref)
    bm = conv_silu(b_ref, gw, gw + n, pb_ref)
    cm = conv_silu(c_ref, gw + n, gw + 2 * n, pc_ref)

    row = lax.broadcasted_iota(jnp.int32, (q, q), 0)
    col = lax.broadcasted_iota(jnp.int32, (q, q), 1)
    tri = row >= col
    acs_g = pltpu.roll(acs_ref[...], (LANES - g * r) % LANES, axis=1)
    dt_t = dtt_ref[...]
    acs_t = acst_ref[...]

    cm_bf = cm.astype(BF16)
    cb = lax.dot_general(cm_bf, bm.astype(BF16), (((1,), (1,)), ((), ())), preferred_element_type=F32)
    bm_t = bm.T

    lane = lax.broadcasted_iota(jnp.int32, (q, LANES), 1)
    lo_half = lane < SSD_HEAD_DIM
    lo_half_row = lo_half[0:1, :]
    pairs = range(r // 2)
    pair_cols = [slice(j * LANES, (j + 1) * LANES) for j in pairs]
    y_off = jnp.dot(cm_bf, h_ref[...].astype(BF16), preferred_element_type=F32)

    lhs_y, lhs_s, rhs, e_pair, cd_pair = [], [], [], [], []
    for j in pairs:
        xs_p = xs[:, pair_cols[j]]
        rhs.append(jnp.concatenate([jnp.where(lo_half, xs_p, 0.0).astype(BF16),
                                    jnp.where(lo_half, 0.0, xs_p).astype(BF16)], axis=0))
        m_parts, bw_parts, e_cols, cd = [], [], [], []
        for hd in (2 * j, 2 * j + 1):
            a_col = acs_g[:, hd:hd + 1]
            a_row = acs_t[hd:hd + 1, :]
            dt_row = dt_t[hd:hd + 1, :]
            a_last = acs_g[q - 1:q, hd:hd + 1]
            lmat = jnp.exp(jnp.where(tri, a_col - a_row, -jnp.inf))
            m_parts.append(cb * lmat * dt_row)
            bw_parts.append(bm_t * (jnp.exp(a_last - a_row) * dt_row))
            e_cols.append(jnp.exp(a_col))
            cd.append(jnp.exp(a_last))
        lhs_y.append(jnp.concatenate(m_parts, axis=1).astype(BF16))
        lhs_s.append(jnp.concatenate(bw_parts, axis=1).astype(BF16))
        e_pair.append(jnp.where(lo_half, e_cols[0], e_cols[1]))
        cd_pair.append(jnp.where(lo_half_row, cd[0], cd[1]))
    y_diag = [jnp.dot(lhs_y[j], rhs[j], preferred_element_type=F32) for j in pairs]
    s_new = [jnp.dot(lhs_s[j], rhs[j], preferred_element_type=F32) for j in pairs]
    ssq = jnp.zeros((q, 1), F32)
    for j in pairs:
        cols = pair_cols[j]
        h_ref[:, cols] = cd_pair[j] * h_ref[:, cols] + s_new[j]
        y = y_diag[j] + e_pair[j] * y_off[:, cols] + px_ref[kw + 1:kw + 2, cols] * xs[:, cols]
        y = y * _silu(z_ref[:, cols].astype(F32))
        ybuf_ref[:, cols] = y
        ssq = ssq + jnp.sum(y * y, axis=1, keepdims=True)
    inv = lax.rsqrt(ssq / gw + RMS_EPS)
    y_ref[...] = (ybuf_ref[...] * inv * px_ref[kw + 2:kw + 3, :]).astype(y_ref.dtype)


def _ssd(zx, acs, dt_t, acs_t, conv_w, conv_b, d_cols, norm_w, *, batch, length, d_inner, groups):
    m = zx.shape[0]
    q = SSD_CHUNK
    assert length % q == 0
    nc = length // q
    gw = d_inner // groups
    r = gw // SSD_HEAD_DIM
    assert r % 2 == 0 and r % SUBLANES == 0 and r * groups <= LANES
    kw = conv_w.shape[0]
    n = SSD_STATE
    conv_dim = conv_w.shape[1]
    rows = jnp.concatenate([conv_w, conv_b], axis=0)
    extra = jnp.zeros((2, conv_dim), F32).at[0, :d_inner].set(d_cols[0]).at[1, :d_inner].set(norm_w[0])
    params = jnp.concatenate([rows, extra], axis=0)
    np_rows = params.shape[0]
    zblk = d_inner // gw
    bblk = 2 * d_inner // n
    cblk_w = d_inner // n
    row = lambda b, g, c: b * nc + c
    in_specs = [
        pl.BlockSpec((q, gw), lambda b, g, c: (row(b, g, c), g)),
        pl.BlockSpec((q, gw), lambda b, g, c: (row(b, g, c), zblk + g)),
        pl.BlockSpec((q, n), lambda b, g, c: (row(b, g, c), bblk + g)),
        pl.BlockSpec((q, n), lambda b, g, c: (row(b, g, c), bblk + groups + g)),
        pl.BlockSpec((q, LANES), lambda b, g, c: (row(b, g, c), 0)),
        pl.BlockSpec((r, q), lambda b, g, c: (g, row(b, g, c))),
        pl.BlockSpec((r, q), lambda b, g, c: (g, row(b, g, c))),
        pl.BlockSpec((np_rows, gw), lambda b, g, c: (0, g)),
        pl.BlockSpec((np_rows, n), lambda b, g, c: (0, cblk_w + g)),
        pl.BlockSpec((np_rows, n), lambda b, g, c: (0, cblk_w + groups + g)),
    ]
    return pl.pallas_call(
        functools.partial(_ssd_body, q=q, r=r, kw=kw),
        grid=(batch, groups, nc),
        in_specs=in_specs,
        out_specs=pl.BlockSpec((q, gw), lambda b, g, c: (row(b, g, c), g)),
        out_shape=jax.ShapeDtypeStruct((m, d_inner), BF16),
        scratch_shapes=[
            pltpu.VMEM((n, gw), F32),
            pltpu.VMEM((SUBLANES, gw + 2 * n), F32),
            pltpu.VMEM((q, gw), F32),
        ],
        compiler_params=_cparams("parallel", "parallel", "arbitrary"),
        name="ssd_scan",
    )(zx, zx, zx, zx, acs, dt_t, acs_t, params, params, params)


def _attn_body(q_ref, k_ref, v_ref, f_ref, *rest, blk, hp, n_cast):
    cast_in, o_ref, cast_out = rest[:n_cast], rest[n_cast], rest[n_cast + 1:2 * n_cast + 1]
    vaug_ref, m_ref, acc_ref = rest[2 * n_cast + 1:]
    for src, dst in zip(cast_in, cast_out):
        dst[...] = src[...].astype(BF16)
    qi = pl.program_id(2)
    dh = FOX_HEAD_DIM
    length = k_ref.shape[0]

    @pl.when(qi == 0)
    def _():
        for h in range(hp):
            vaug_ref[h, :, 0:dh] = v_ref[:, h * dh:(h + 1) * dh]
            vaug_ref[h, :, dh:2 * dh] = jnp.ones((length, dh), BF16)

    m_ref[...] = jnp.full_like(m_ref, -jnp.inf)
    acc_ref[...] = jnp.zeros_like(acc_ref)

    def step(j, masked):
        start = pl.multiple_of(j * blk, blk)

        def scores(h):
            qv = q_ref[:, h * dh:(h + 1) * dh]
            kj = k_ref[pl.ds(start, blk), h * dh:(h + 1) * dh]
            return lax.dot_general(qv, kj, (((1,), (1,)), ((), ())), preferred_element_type=F32)

        if masked:
            causal = (lax.broadcasted_iota(jnp.int32, (blk, blk), 0)
                      >= lax.broadcasted_iota(jnp.int32, (blk, blk), 1))

        def softmax_part(h, s):
            s = s - f_ref[h, pl.ds(j, 1), :]
            if masked:
                s = jnp.where(causal, s, -jnp.inf)
            m_old = m_ref[h]
            m_new = jnp.maximum(m_old, jnp.max(s, axis=1, keepdims=True))
            m_ref[h] = m_new
            return jnp.exp(s - jnp.tile(m_new, (1, blk // LANES))).astype(BF16), jnp.exp(m_old - m_new)

        def accumulate(h, p, alpha):
            pv = jnp.dot(p, vaug_ref[h, pl.ds(start, blk), :], preferred_element_type=F32)
            acc_ref[h] = jnp.tile(alpha, (1, 2)) * acc_ref[h] + pv

        s_cur = scores(0)
        pending = None
        for h in range(hp):
            s_nxt = scores(h + 1) if h + 1 < hp else None
            p, alpha = softmax_part(h, s_cur)
            if pending is not None:
                accumulate(*pending)
            pending = (h, p, alpha)
            s_cur = s_nxt
        accumulate(*pending)

    def loop_body(j, carry):
        step(j, False)
        return carry

    lax.fori_loop(0, qi, loop_body, 0)
    step(qi, True)
    for h in range(hp):
        acc = acc_ref[h]
        o_ref[:, h * dh:(h + 1) * dh] = (acc[:, 0:dh] / acc[:, dh:2 * dh]).astype(o_ref.dtype)


def _attention(qkv, fcum_t, cast_weights, *, batch, length, heads):
    m = qkv.shape[0]
    dh = FOX_HEAD_DIM
    hp = ATT_HEADS_PER_STEP
    assert dh == LANES and heads % hp == 0
    blk = _tile(length, ATT_BLOCK)
    nq = length // blk
    ng = heads // hp
    fcum_t = fcum_t.reshape(batch * ng, hp, nq, blk)
    cast_specs, cast_shapes = _cast_slices(cast_weights, batch * ng * nq, lambda b, g, i: (b * ng + g) * nq + i)
    outs = pl.pallas_call(
        functools.partial(_attn_body, blk=blk, hp=hp, n_cast=len(cast_weights)),
        grid=(batch, ng, nq),
        in_specs=[
            pl.BlockSpec((blk, hp * dh), lambda b, g, i: (b * nq + i, g)),
            pl.BlockSpec((length, hp * dh), lambda b, g, i: (b, ng + g)),
            pl.BlockSpec((length, hp * dh), lambda b, g, i: (b, 2 * ng + g)),
            pl.BlockSpec((None, hp, nq, blk), lambda b, g, i: (b * ng + g, 0, 0, 0)),
        ] + cast_specs,
        out_specs=[pl.BlockSpec((blk, hp * dh), lambda b, g, i: (b * nq + i, g))] + cast_specs,
        out_shape=[jax.ShapeDtypeStruct((m, heads * dh), BF16)] + cast_shapes,
        scratch_shapes=[pltpu.VMEM((hp, length, 2 * dh), BF16),
                        pltpu.VMEM((hp, blk, LANES), F32),
                        pltpu.VMEM((hp, blk, 2 * dh), F32)],
        compiler_params=_cparams("parallel", "parallel", "arbitrary"),
        name="fox_attention",
    )(qkv, qkv, qkv, fcum_t, *cast_weights)
    return outs[0], outs[1:]


def _merge_body(ys_ref, ya_ref, ws_ref, wa_ref, gs_ref, ga_ref, cast_in, o_ref, cast_out):
    cast_out[...] = cast_in[...].astype(BF16)
    ps = jnp.dot(ys_ref[...], ws_ref[...], preferred_element_type=F32)
    pa = jnp.dot(ya_ref[...], wa_ref[...], preferred_element_type=F32)
    o_ref[...] = (gs_ref[...].astype(F32) * ps + ga_ref[...].astype(F32) * pa).astype(o_ref.dtype)


def _merge(y_ssd, y_att, w_ssd, w_att, gates, cast_weight, *, tm=1024, tn=256):
    m, ks = y_ssd.shape
    ka = y_att.shape[1]
    n = w_ssd.shape[1]
    tm, tn = _tile(m, tm), _tile(n, tn)
    nj = n // tn
    once = pl.Buffered(1)
    cast_specs, cast_shapes = _cast_slices([cast_weight], (m // tm) * nj, lambda i, j: i * nj + j)
    return pl.pallas_call(
        _merge_body,
        grid=(m // tm, nj),
        in_specs=[
            pl.BlockSpec((tm, ks), lambda i, j: (i, 0), pipeline_mode=once),
            pl.BlockSpec((tm, ka), lambda i, j: (i, 0), pipeline_mode=once),
            pl.BlockSpec((ks, tn), lambda i, j: (0, j)),
            pl.BlockSpec((ka, tn), lambda i, j: (0, j)),
            pl.BlockSpec((tm, tn), lambda i, j: (i, j)),
            pl.BlockSpec((tm, tn), lambda i, j: (i, nj + j)),
        ] + cast_specs,
        out_specs=[pl.BlockSpec((tm, tn), lambda i, j: (i, j))] + cast_specs,
        out_shape=[jax.ShapeDtypeStruct((m, n), BF16)] + cast_shapes,
        compiler_params=_cparams("parallel", "arbitrary"),
        name="merge_proj",
    )(y_ssd, y_att, w_ssd, w_att, gates, gates, cast_weight)


def _mm_residual_body(a_ref, w_ref, r_ref, o_ref, *, alpha):
    acc = jnp.dot(a_ref[...], w_ref[...], preferred_element_type=F32)
    o_ref[...] = alpha * r_ref[...] + acc


def _matmul_residual(a, w, resid, alpha, *, tm, tn, lhs_buffers=2, name):
    m, k = a.shape
    n = w.shape[1]
    tm, tn = _tile(m, tm), _tile(n, tn)
    return pl.pallas_call(
        functools.partial(_mm_residual_body, alpha=alpha),
        grid=(m // tm, n // tn),
        in_specs=[
            pl.BlockSpec((tm, k), lambda i, j: (i, 0), pipeline_mode=pl.Buffered(lhs_buffers)),
            pl.BlockSpec((k, tn), lambda i, j: (0, j)),
            pl.BlockSpec((tm, tn), lambda i, j: (i, j)),
        ],
        out_specs=pl.BlockSpec((tm, tn), lambda i, j: (i, j)),
        out_shape=jax.ShapeDtypeStruct((m, n), F32),
        compiler_params=_cparams("parallel", "arbitrary"),
        name=name,
    )(a, w, resid)


def _ln_body(x_ref, g_ref, b_ref, *o_refs):
    x = x_ref[...]
    mu = jnp.mean(x, axis=-1, keepdims=True)
    xc = x - mu
    var = jnp.mean(xc * xc, axis=-1, keepdims=True)
    out = xc * lax.rsqrt(var + LN_EPS) * g_ref[...] + b_ref[...]
    for o_ref in o_refs:
        o_ref[...] = out.astype(o_ref.dtype)


def _layer_norm(x, gain, bias, out_dtypes, *, tm=256, name):
    m, d = x.shape
    tm = _tile(m, tm, SUBLANES)
    outs = pl.pallas_call(
        _ln_body,
        grid=(m // tm,),
        in_specs=[pl.BlockSpec((tm, d), lambda i: (i, 0)),
                  pl.BlockSpec((1, d), lambda i: (0, 0)),
                  pl.BlockSpec((1, d), lambda i: (0, 0))],
        out_specs=[pl.BlockSpec((tm, d), lambda i: (i, 0)) for _ in out_dtypes],
        out_shape=[jax.ShapeDtypeStruct((m, d), dt) for dt in out_dtypes],
        compiler_params=_cparams("parallel"),
        name=name,
    )(x, gain, bias)
    return outs


def _ffn_up_body(a_ref, halo_ref, wv_ref, wg_ref, cwv_ref, cwg_ref, cbv_ref, cbg_ref, cast_in, o_ref, cast_out,
                 wbf_ref, *, kw, tiles_per_seq):
    i = pl.program_id(1)
    pad = BF16_SUBLANES
    cast_out[...] = cast_in[...].astype(BF16)

    @pl.when(i == 0)
    def _():
        wbf_ref[0] = wv_ref[...].astype(BF16)
        wbf_ref[1] = wg_ref[...].astype(BF16)

    a = a_ref[...]
    halo = halo_ref[...]
    keep = (i % tiles_per_seq != 0).astype(F32)

    def conv(c, cw_ref, cb_ref):
        w = wbf_ref[c]
        u = jnp.dot(a, w, preferred_element_type=F32)
        uh = jnp.dot(halo, w, preferred_element_type=F32) * keep
        cw = cw_ref[...]
        cb = cb_ref[...]
        head_src = jnp.concatenate([uh, u[0:pad, :]], axis=0)
        head = cb + cw[kw - 1:kw, :] * head_src[pad:2 * pad, :]
        out = cw[0:1, :] * u
        for k in range(kw - 1):
            shift = kw - 1 - k
            out = cw[k + 1:k + 2, :] * u + pltpu.roll(out, 1, axis=0)
            head = head + cw[k:k + 1, :] * head_src[pad - shift:2 * pad - shift, :]
        return cb + out, head

    gate, gate_head = conv(1, cwg_ref, cbg_ref)
    gate, gate_head = _silu(gate), _silu(gate_head)
    val, val_head = conv(0, cwv_ref, cbv_ref)
    o_ref[...] = (gate * val).astype(o_ref.dtype)
    o_ref[0:pad, :] = (gate_head * val_head).astype(o_ref.dtype)


def _ffn_up(h, w_up, conv_w, conv_b, cast_weight, *, length, d_ff, tm=1024, tn=256):
    m, k = h.shape
    tm = _tile(min(m, length), tm, BF16_SUBLANES)
    assert length % tm == 0
    tn = _tile(d_ff, tn)
    nj = d_ff // tn
    ni = m // tm
    kw = conv_w.shape[0]
    hb = tm // BF16_SUBLANES
    cast_specs, cast_shapes = _cast_slices([cast_weight], nj * ni, lambda j, i: j * ni + i)
    return pl.pallas_call(
        functools.partial(_ffn_up_body, kw=kw, tiles_per_seq=length // tm),
        grid=(nj, m // tm),
        in_specs=[
            pl.BlockSpec((tm, k), lambda j, i: (i, 0)),
            pl.BlockSpec((BF16_SUBLANES, k), lambda j, i: (jnp.maximum(i * hb - 1, 0), 0)),
            pl.BlockSpec((k, tn), lambda j, i: (0, j)),
            pl.BlockSpec((k, tn), lambda j, i: (0, nj + j)),
            pl.BlockSpec((kw, tn), lambda j, i: (0, j)),
            pl.BlockSpec((kw, tn), lambda j, i: (0, nj + j)),
            pl.BlockSpec((1, tn), lambda j, i: (0, j)),
            pl.BlockSpec((1, tn), lambda j, i: (0, nj + j)),
        ] + cast_specs,
        out_specs=[pl.BlockSpec((tm, tn), lambda j, i: (i, j))] + cast_specs,
        out_shape=[jax.ShapeDtypeStruct((m, d_ff), BF16)] + cast_shapes,
        scratch_shapes=[pltpu.VMEM((2, k, tn), BF16)],
        compiler_params=_cparams("parallel", "arbitrary"),
        name="ffn_up_conv_act",
    )(h, h, w_up, w_up, conv_w, conv_w, conv_b, conv_b, cast_weight)


def _layer(h, p, *, batch, length, alpha):
    m, d = h.shape
    d_inner = p["ssd_norm_w"].shape[-1]
    conv_dim = p["ssd_conv_b"].shape[-1]
    ssd_heads = p["ssd_dt_bias"].shape[-1]
    fox_heads = p["fox_f_bias"].shape[-1]
    d_att = fox_heads * FOX_HEAD_DIM
    d_ff = p["w_down"].shape[0]
    groups = (conv_dim - d_inner) // (2 * SSD_STATE)
    assert ssd_heads <= LANES and fox_heads <= LANES

    o_z, o_xbc = 0, d_inner
    o_dt = o_xbc + conv_dim
    o_q = o_dt + ssd_heads
    o_f = o_q + 3 * d_att
    o_g = o_f + fox_heads
    wt_f32 = p["w_in"].T
    wt_head = _cast_rows(wt_f32, 0, o_q)
    zeros = lambda n: jnp.zeros((n, d), BF16)
    wt_small = jnp.concatenate([wt_head[o_dt:o_q], zeros(LANES - ssd_heads),
                                _cast_rows(wt_f32, o_f, fox_heads), zeros(LANES - fox_heads)], axis=0)

    h_bf = h.astype(BF16)
    zx, wt_qkv, wt_gate = _proj(h_bf, wt_head, F32, row0=o_z, n=o_dt,
                                casts=[(wt_f32, o_q, 3 * d_att), (wt_f32, o_g, 2 * d)], lhs_buffers=1,
                                name="in_proj_zx")
    qkv = _proj(h_bf, wt_qkv, BF16, scaled_cols=d_att, scale=1.0 / math.sqrt(FOX_HEAD_DIM), name="in_proj_qkv")
    gates = _proj(h_bf, wt_gate, F32, bias=p["gate_bias"].reshape(1, 2 * d), name="in_proj_gates")

    pad_row = lambda v, n: jnp.pad(v.reshape(1, -1).astype(F32), ((0, 0), (0, n - v.shape[-1])))
    acs, dt_t, acs_t, fcum = _head_prep(h_bf, wt_small, pad_row(p["ssd_dt_bias"], LANES),
                                        pad_row(p["ssd_a_log"], LANES), pad_row(p["fox_f_bias"], LANES),
                                        batch=batch, length=length, fox_heads=fox_heads)
    y_ssd = _ssd(zx, acs, dt_t, acs_t, p["ssd_conv_w"], p["ssd_conv_b"].reshape(1, -1),
                 jnp.repeat(p["ssd_d"].astype(F32), SSD_HEAD_DIM).reshape(1, -1),
                 p["ssd_norm_w"].reshape(1, -1),
                 batch=batch, length=length, d_inner=d_inner, groups=groups)
    y_att, (w_ssd_bf, w_att_bf) = _attention(qkv, fcum, [p["w_proj_ssd"], p["w_proj_att"]],
                                             batch=batch, length=length, heads=fox_heads)

    merged, w_out_bf = _merge(y_ssd, y_att, w_ssd_bf, w_att_bf, gates, p["w_out"])
    pre1 = _matmul_residual(merged, w_out_bf, h, alpha, tm=1024, tn=512, name="out_proj_residual")
    h1, h1_bf = _layer_norm(pre1, p["ln1_g"].reshape(1, -1), p["ln1_b"].reshape(1, -1), (F32, BF16), name="layer_norm_1")

    act, w_down_bf = _ffn_up(h1_bf, p["w_up"], p["ffn_conv_w"], p["ffn_conv_b"].reshape(1, -1), p["w_down"],
                             length=length, d_ff=d_ff)
    pre2 = _matmul_residual(act, w_down_bf, h1, alpha, tm=1024, tn=256, lhs_buffers=1,
                            name="ffn_down_residual")
    (out,) = _layer_norm(pre2, p["ln2_g"].reshape(1, -1), p["ln2_b"].reshape(1, -1), (F32,), name="layer_norm_2")
    return out


_PARAM_NAMES = ("w_in", "ssd_conv_w", "ssd_conv_b", "ssd_dt_bias", "ssd_a_log", "ssd_d", "ssd_norm_w",
                "fox_f_bias", "gate_bias", "w_proj_ssd", "w_proj_att", "w_out", "ln1_g", "ln1_b",
                "w_up", "ffn_conv_w", "ffn_conv_b", "w_down", "ln2_g", "ln2_b")


def kernel(x, w_in, ssd_conv_w, ssd_conv_b, ssd_dt_bias, ssd_a_log, ssd_d, ssd_norm_w, fox_f_bias, gate_bias,
           w_proj_ssd, w_proj_att, w_out, ln1_g, ln1_b, w_up, ffn_conv_w, ffn_conv_b, w_down, ln2_g, ln2_b):
    params = (w_in, ssd_conv_w, ssd_conv_b, ssd_dt_bias, ssd_a_log, ssd_d, ssd_norm_w, fox_f_bias, gate_bias,
              w_proj_ssd, w_proj_att, w_out, ln1_g, ln1_b, w_up, ffn_conv_w, ffn_conv_b, w_down, ln2_g, ln2_b)
    batch, length, d = x.shape
    depth = w_in.shape[0]
    alpha = (2.0 * depth) ** 0.25
    h = x.reshape(batch * length, d)
    for layer in range(depth):
        p = {name: arr[layer] for name, arr in zip(_PARAM_NAMES, params)}
        h = _layer(h, p, batch=batch, length=length, alpha=alpha)
    return h.reshape(batch, length, d)
```
